```python
import jax, jax.numpy as jnp
from jax import lax
import numpy as np

D_MODEL = 2048
BATCH = 2
SEQ = 16384
DEPTH = 2

CHUNK = 64
N_MIXERS = 2
SB_HEADS = 8
SB_HEAD_DIM = 128
SB_INNER = SB_HEADS * SB_HEAD_DIM
SB_SCALE = SB_HEAD_DIM ** -0.5
Q_BLOCK = 128
POOL_WINDOWS = (2, 4, 8, 16)
POOL_GROUPS = len(POOL_WINDOWS)
POOL_GROUP_DIM = D_MODEL // POOL_GROUPS
D_FF = 2 * D_MODEL
DEEPNORM_ALPHA = (2 * DEPTH) ** 0.25
DEEPNORM_BETA = (8 * DEPTH) ** -0.25
FFN_RESIDUAL = 0.5
LN_EPS = 1e-5

kernel_name = 'hybrid_stickbreak_pool_macaron_deepnorm'


def layer_norm(x, g, b):
    xf = x.astype(jnp.float32)
    mu = jnp.mean(xf, axis=-1, keepdims=True)
    var = jnp.mean(jnp.square(xf - mu), axis=-1, keepdims=True)
    y = (xf - mu) * lax.rsqrt(var + LN_EPS)
    return (y * g.astype(jnp.float32) + b.astype(jnp.float32)).astype(x.dtype)


def swiglu(x, w_in, w_out):
    a, gate = jnp.split(x @ w_in, 2, axis=-1)
    return (jax.nn.silu(a) * gate) @ w_out


def stick_breaking_attention(x, w_qkv, w_o):
    bsz, seq, _ = x.shape
    n_blk = seq // Q_BLOCK
    q, k, v = jnp.split(x @ w_qkv, 3, axis=-1)
    heads = lambda t: t.reshape(bsz, seq, SB_HEADS, SB_HEAD_DIM).transpose(0, 2, 1, 3)
    q, k, v = heads(q), heads(k), heads(v)
    outs = []
    for blk in range(n_blk):
        start, end = blk * Q_BLOCK, (blk + 1) * Q_BLOCK
        q_blk = q[:, :, start:end]
        k_past, v_past = k[:, :, :end], v[:, :, :end]
        z = jnp.einsum('bhqd,bhkd->bhqk', q_blk, k_past).astype(jnp.float32) * SB_SCALE
        past = jnp.arange(end)[None, :] < (start + jnp.arange(Q_BLOCK))[:, None]
        log_keep = jnp.where(past, jax.nn.log_sigmoid(-z), 0.0)
        log_w = z + lax.cumsum(log_keep, axis=3, reverse=True)
        weights = jnp.where(past, jnp.exp(jnp.minimum(log_w, 0.0)), 0.0)
        outs.append(jnp.einsum('bhqk,bhkd->bhqd', weights.astype(v.dtype), v_past))
    o = jnp.concatenate(outs, axis=2)
    o = o.transpose(0, 2, 1, 3).reshape(bsz, seq, SB_INNER)
    return o @ w_o


def causal_window_mean_minus_self(u, window):
    seq = u.shape[1]
    csum = jnp.cumsum(u.astype(jnp.float32), axis=1)
    lagged = jnp.pad(csum, ((0, 0), (window, 0), (0, 0)))[:, :seq]
    count = jnp.minimum(jnp.arange(1, seq + 1), window).astype(jnp.float32)
    return ((csum - lagged) / count[None, :, None] - u.astype(jnp.float32)).astype(u.dtype)


def pool_mixer(x, w_in, w_grp, scale, w_out):
    bsz, seq, _ = x.shape
    u, g = jnp.split(x @ w_in, 2, axis=-1)
    groups = jnp.split(u, POOL_GROUPS, axis=-1)
    pooled = jnp.stack([causal_window_mean_minus_self(ug, w) for ug, w in zip(groups, POOL_WINDOWS)], axis=2)
    mixed = jnp.einsum('bsgc,gcd->bsgd', pooled, w_grp).reshape(bsz, seq, D_MODEL)
    return (scale * mixed * jax.nn.silu(g)) @ w_out


def setup_inputs(seed: int = 0) -> dict:
    key = jax.random.key(seed)
    keys = iter(jax.random.split(key, 64))
    f32 = jnp.float32

    def dense(fan_in, shape, gain=1.0):
        return jax.random.normal(next(keys), shape, f32) * (gain * fan_in ** -0.5)

    def gain_vec(n, noise):
        return 1.0 + noise * jax.random.normal(next(keys), (n,), f32)

    def bias_vec(n):
        return 0.02 * jax.random.normal(next(keys), (n,), f32)

    inputs = {'x': jax.random.normal(next(keys), (BATCH, SEQ, D_MODEL), f32)}
    for i in range(DEPTH):
        p = f'l{i}_'
        inputs[p + 'ffn1_w_in'] = dense(D_MODEL, (D_MODEL, 2 * D_FF))
        inputs[p + 'ffn1_w_out'] = dense(D_FF, (D_FF, D_MODEL), DEEPNORM_BETA)
        inputs[p + 'ln1_g'] = gain_vec(D_MODEL, 0.02)
        inputs[p + 'ln1_b'] = bias_vec(D_MODEL)
        if i % N_MIXERS == 0:
            inputs[p + 'sb_w_qkv'] = dense(D_MODEL, (D_MODEL, 3 * SB_INNER))
            inputs[p + 'sb_w_o'] = dense(SB_INNER, (SB_INNER, D_MODEL), DEEPNORM_BETA)
        else:
            inputs[p + 'pool_w_in'] = dense(D_MODEL, (D_MODEL, 2 * D_MODEL))
            inputs[p + 'pool_w_grp'] = dense(POOL_GROUP_DIM, (POOL_GROUPS, POOL_GROUP_DIM, POOL_GROUP_DIM))
            inputs[p + 'pool_scale'] = gain_vec(D_MODEL, 0.1)
            inputs[p + 'pool_w_out'] = dense(D_MODEL, (D_MODEL, D_MODEL), DEEPNORM_BETA)
        inputs[p + 'ln2_g'] = gain_vec(D_MODEL, 0.02)
        inputs[p + 'ln2_b'] = bias_vec(D_MODEL)
        inputs[p + 'ffn2_w_in'] = dense(D_MODEL, (D_MODEL, 2 * D_FF))
        inputs[p + 'ffn2_w_out'] = dense(D_FF, (D_FF, D_MODEL), DEEPNORM_BETA)
        inputs[p + 'ln3_g'] = gain_vec(D_MODEL, 0.02)
        inputs[p + 'ln3_b'] = bias_vec(D_MODEL)
    return inputs


def reference(x,
              l0_ffn1_w_in, l0_ffn1_w_out, l0_ln1_g, l0_ln1_b,
              l0_sb_w_qkv, l0_sb_w_o, l0_ln2_g, l0_ln2_b,
              l0_ffn2_w_in, l0_ffn2_w_out, l0_ln3_g, l0_ln3_b,
              l1_ffn1_w_in, l1_ffn1_w_out, l1_ln1_g, l1_ln1_b,
              l1_pool_w_in, l1_pool_w_grp, l1_pool_scale, l1_pool_w_out, l1_ln2_g, l1_ln2_b,
              l1_ffn2_w_in, l1_ffn2_w_out, l1_ln3_g, l1_ln3_b):
    ffn1 = [(l0_ffn1_w_in, l0_ffn1_w_out), (l1_ffn1_w_in, l1_ffn1_w_out)]
    ffn2 = [(l0_ffn2_w_in, l0_ffn2_w_out), (l1_ffn2_w_in, l1_ffn2_w_out)]
    ln1 = [(l0_ln1_g, l0_ln1_b), (l1_ln1_g, l1_ln1_b)]
    ln2 = [(l0_ln2_g, l0_ln2_b), (l1_ln2_g, l1_ln2_b)]
    ln3 = [(l0_ln3_g, l0_ln3_b), (l1_ln3_g, l1_ln3_b)]
    mixers = [
        lambda h: stick_breaking_attention(h, l0_sb_w_qkv, l0_sb_w_o),
        lambda h: pool_mixer(h, l1_pool_w_in, l1_pool_w_grp, l1_pool_scale, l1_pool_w_out),
    ]
    for i in range(DEPTH):
        x = layer_norm(DEEPNORM_ALPHA * x + FFN_RESIDUAL * swiglu(x, *ffn1[i]), *ln1[i])
        x = layer_norm(DEEPNORM_ALPHA * x + mixers[i](x), *ln2[i])
        x = layer_norm(DEEPNORM_ALPHA * x + FFN_RESIDUAL * swiglu(x, *ffn2[i]), *ln3[i])
    return x
```

```python
import functools

import jax
import jax.numpy as jnp
from jax import lax
from jax.experimental import pallas as pl
from jax.experimental.pallas import tpu as pltpu

F32 = jnp.float32
BF16 = jnp.bfloat16

DEPTH = 2
DEEPNORM_ALPHA = (2 * DEPTH) ** 0.25
FFN_RESIDUAL = 0.5
LN_EPS = 1e-5
SB_HEADS = 8
POOL_WINDOWS = (2, 4, 8, 16)
POOL_HALO = 128

F32_EXP_ZERO_BELOW = -104.0
CUTOFF_MARGIN = 1.0

VMEM_LIMIT = 56 * 1024 * 1024


def _params(semantics):
    return pltpu.CompilerParams(dimension_semantics=semantics, vmem_limit_bytes=VMEM_LIMIT)


def _layer_norm(y, g, b):
    mu = jnp.mean(y, axis=-1, keepdims=True)
    d = y - mu
    var = jnp.mean(d * d, axis=-1, keepdims=True)
    return d * lax.rsqrt(var + LN_EPS) * g + b


def _silu(a):
    return a * jax.nn.sigmoid(a)


def _dot(a, b):
    return jnp.dot(a, b, preferred_element_type=F32)


def _split_bf16(x):
    hi = x.astype(BF16)
    lo = (x - hi.astype(F32)).astype(BF16)
    return hi, lo


def _ffn_ln_kernel(x_ref, wa_ref, wg_ref, wo_ref, g_ref, b_ref, o_ref, xb_ref):
    j = pl.program_id(1)

    @pl.when(j == 0)
    def _():
        xb_ref[...] = x_ref[...].astype(BF16)
        o_ref[...] = jnp.zeros_like(o_ref)

    xb = xb_ref[...]
    a = _dot(xb, wa_ref[...])
    gate = _dot(xb, wg_ref[...])
    h = (_silu(a) * gate).astype(BF16)
    o_ref[...] += _dot(h, wo_ref[...])

    @pl.when(j == pl.num_programs(1) - 1)
    def _():
        y = DEEPNORM_ALPHA * x_ref[...] + FFN_RESIDUAL * o_ref[...]
        o_ref[...] = _layer_norm(y, g_ref[...], b_ref[...])


def _ffn_ln(x, w_in, w_out, g, b, *, tm, tf):
    n, d = x.shape
    d_ff = w_out.shape[0]
    nj = d_ff // tf
    return pl.pallas_call(
        _ffn_ln_kernel,
        grid=(n // tm, nj),
        in_specs=[
            pl.BlockSpec((tm, d), lambda i, j: (i, 0)),
            pl.BlockSpec((d, tf), lambda i, j: (0, j)),
            pl.BlockSpec((d, tf), lambda i, j: (0, nj + j)),
            pl.BlockSpec((tf, d), lambda i, j: (j, 0)),
            pl.BlockSpec((1, d), lambda i, j: (0, 0)),
            pl.BlockSpec((1, d), lambda i, j: (0, 0)),
        ],
        out_specs=pl.BlockSpec((tm, d), lambda i, j: (i, 0)),
        out_shape=jax.ShapeDtypeStruct((n, d), F32),
        scratch_shapes=[pltpu.VMEM((tm, d), BF16)],
        compiler_params=_params(("arbitrary", "arbitrary")),
        name="ffn_ln",
    )(x, w_in, w_in, w_out, g, b)


def _proj_kernel(x_ref, w_ref, o_ref, xb_ref):
    @pl.when(pl.program_id(1) == 0)
    def _():
        xb_ref[...] = x_ref[...].astype(BF16)

    o_ref[...] = _dot(xb_ref[...], w_ref[...]).astype(o_ref.dtype)


def _proj(x, w, out_dtype, *, tm, tn):
    n, d = x.shape
    n_out = w.shape[1]
    return pl.pallas_call(
        _proj_kernel,
        grid=(n // tm, n_out // tn),
        in_specs=[
            pl.BlockSpec((tm, d), lambda i, j: (i, 0)),
            pl.BlockSpec((d, tn), lambda i, j: (0, j)),
        ],
        out_specs=pl.BlockSpec((tm, tn), lambda i, j: (i, j)),
        out_shape=jax.ShapeDtypeStruct((n, n_out), out_dtype),
        scratch_shapes=[pltpu.VMEM((tm, d), BF16)],
        compiler_params=_params(("arbitrary", "arbitrary")),
        name="proj",
    )(x, w)


def _sb_attn_kernel(q_ref, k_ref, v_ref, o_ref, kmax2_ref, *, t_blk, scale):
    qi = pl.program_id(2)
    n_kblk = k_ref.shape[0] // t_blk

    @pl.when(qi == 0)
    def _():
        def body(c, m):
            kf = k_ref[pl.ds(pl.multiple_of(c * t_blk, t_blk), t_blk), :].astype(F32)
            return jnp.maximum(m, jnp.max(jnp.sum(kf * kf, axis=-1, keepdims=True)))

        kmax2_ref[0] = lax.fori_loop(0, n_kblk, body, jnp.float32(0.0))

    q = q_ref[...]
    qf = q.astype(F32)
    z_bound = jnp.sqrt(jnp.sum(qf * qf, axis=-1, keepdims=True) * kmax2_ref[0]) * scale

    row = lax.broadcasted_iota(jnp.int32, (t_blk, t_blk), 0)
    col = lax.broadcasted_iota(jnp.int32, (t_blk, t_blk), 1)
    suffix_ones = (row >= col).astype(BF16)
    past = col < row

    def sweep(k_start, c, acc, diagonal):
        k = k_ref[pl.ds(k_start, t_blk), :]
        v = v_ref[pl.ds(k_start, t_blk), :]
        z = lax.dot_general(q, k, (((1,), (1,)), ((), ())), preferred_element_type=F32) * scale
        log_keep = -(jnp.maximum(z, 0.0) + jnp.log1p(jnp.exp(-jnp.abs(z))))
        if diagonal:
            log_keep = jnp.where(past, log_keep, 0.0)
        hi, lo = _split_bf16(log_keep)
        tail = _dot(hi, suffix_ones) + _dot(lo, suffix_ones)
        w = jnp.exp(jnp.minimum(z + tail + c, 0.0))
        if diagonal:
            w = jnp.where(past, w, 0.0)
        acc = acc + _dot(w.astype(BF16), v)
        return c + tail[:, 0:1], acc

    def live(c):
        return jnp.max(z_bound + c) >= F32_EXP_ZERO_BELOW - CUTOFF_MARGIN

    c0 = jnp.zeros((t_blk, 1), F32)
    acc0 = jnp.zeros(o_ref.shape, F32)
    c, acc = sweep(pl.multiple_of(qi * t_blk, t_blk), c0, acc0, True)

    def cond(carry):
        kb, go, _, _ = carry
        return jnp.logical_and(kb >= 0, go)

    def body(carry):
        kb, _, c, acc = carry
        c, acc = sweep(pl.multiple_of(kb * t_blk, t_blk), c, acc, False)
        return kb - 1, live(c), c, acc

    _, _, _, acc = lax.while_loop(cond, body, (qi - 1, live(c), c, acc))
    o_ref[...] = acc.astype(o_ref.dtype)


def _sb_attention(qkv, *, bsz, seq, t_blk):
    n, three_inner = qkv.shape
    inner = three_inner // 3
    dh = inner // SB_HEADS
    nq = seq // t_blk
    return pl.pallas_call(
        functools.partial(_sb_attn_kernel, t_blk=t_blk, scale=dh ** -0.5),
        grid=(bsz, SB_HEADS, nq),
        in_specs=[
            pl.BlockSpec((t_blk, dh), lambda b, h, i: (b * nq + i, h)),
            pl.BlockSpec((seq, dh), lambda b, h, i: (b, SB_HEADS + h)),
            pl.BlockSpec((seq, dh), lambda b, h, i: (b, 2 * SB_HEADS + h)),
        ],
        out_specs=pl.BlockSpec((t_blk, dh), lambda b, h, i: (b * nq + i, h)),
        out_shape=jax.ShapeDtypeStruct((n, inner), BF16),
        scratch_shapes=[pltpu.SMEM((1,), F32)],
        compiler_params=_params(("arbitrary", "arbitrary", "arbitrary")),
        name="sb_attn",
    )(qkv, qkv, qkv)


def _proj_res_ln_kernel(o_ref, w_ref, x_ref, g_ref, b_ref, out_ref):
    y = DEEPNORM_ALPHA * x_ref[...] + _dot(o_ref[...], w_ref[...])
    out_ref[...] = _layer_norm(y, g_ref[...], b_ref[...])


def _proj_res_ln(o, w, x, g, b, *, tm):
    n, d = x.shape
    inner = o.shape[1]
    return pl.pallas_call(
        _proj_res_ln_kernel,
        grid=(n // tm,),
        in_specs=[
            pl.BlockSpec((tm, inner), lambda i: (i, 0)),
            pl.BlockSpec((inner, d), lambda i: (0, 0)),
            pl.BlockSpec((tm, d), lambda i: (i, 0)),
            pl.BlockSpec((1, d), lambda i: (0, 0)),
            pl.BlockSpec((1, d), lambda i: (0, 0)),
        ],
        out_specs=pl.BlockSpec((tm, d), lambda i: (i, 0)),
        out_shape=jax.ShapeDtypeStruct((n, d), F32),
        compiler_params=_params(("arbitrary",)),
        name="proj_res_ln",
    )(o, w, x, g, b)


def _pool_mix_kernel(u_ref, halo_ref, gate_ref, x_ref, wgrp_ref, scale_ref, wout_ref, g_ref, b_ref,
                     out_ref, t_ref, *, tiles_per_seq):
    tm, d = x_ref.shape
    cg = d // len(POOL_WINDOWS)
    tile_in_seq = pl.program_id(0) % tiles_per_seq

    r = lax.broadcasted_iota(jnp.int32, (tm, tm + POOL_HALO), 0) + POOL_HALO
    c = lax.broadcasted_iota(jnp.int32, (tm, tm + POOL_HALO), 1)
    pos = tile_in_seq * tm + lax.broadcasted_iota(jnp.int32, (tm, 1), 0)
    halo_in_seq = tile_in_seq * tm - POOL_HALO + lax.broadcasted_iota(jnp.int32, (POOL_HALO, 1), 0) >= 0

    for gi, window in enumerate(POOL_WINDOWS):
        cols = slice(gi * cg, (gi + 1) * cg)
        u = u_ref[:, cols]
        halo = jnp.where(halo_in_seq, halo_ref[:, cols], 0.0)
        hi, lo = _split_bf16(jnp.concatenate([halo, u], axis=0))
        band = jnp.logical_and(c > r - window, c <= r).astype(BF16)
        wsum = _dot(band, hi) + _dot(band, lo)
        count = jnp.minimum(pos + 1, window).astype(F32)
        pooled = wsum / count - u
        mixed = _dot(pooled.astype(BF16), wgrp_ref[gi])
        t_ref[:, cols] = (scale_ref[:, cols] * mixed * _silu(gate_ref[:, cols])).astype(BF16)

    y = DEEPNORM_ALPHA * x_ref[...] + _dot(t_ref[...], wout_ref[...])
    out_ref[...] = _layer_norm(y, g_ref[...], b_ref[...])


def _pool_mix(ug, x, w_grp, scale, w_out, g, b, *, seq, tm):
    n, d = x.shape
    groups, cg, _ = w_grp.shape
    halo_per_tile = tm // POOL_HALO
    return pl.pallas_call(
        functools.partial(_pool_mix_kernel, tiles_per_seq=seq // tm),
        grid=(n // tm,),
        in_specs=[
            pl.BlockSpec((tm, d), lambda i: (i, 0)),
            pl.BlockSpec((POOL_HALO, d), lambda i: (jnp.maximum(i * halo_per_tile - 1, 0), 0)),
            pl.BlockSpec((tm, d), lambda i: (i, 1)),
            pl.BlockSpec((tm, d), lambda i: (i, 0)),
            pl.BlockSpec((groups, cg, cg), lambda i: (0, 0, 0)),
            pl.BlockSpec((1, d), lambda i: (0, 0)),
            pl.BlockSpec((d, d), lambda i: (0, 0)),
            pl.BlockSpec((1, d), lambda i: (0, 0)),
            pl.BlockSpec((1, d), lambda i: (0, 0)),
        ],
        out_specs=pl.BlockSpec((tm, d), lambda i: (i, 0)),
        out_shape=jax.ShapeDtypeStruct((n, d), F32),
        scratch_shapes=[pltpu.VMEM((tm, d), BF16)],
        compiler_params=_params(("arbitrary",)),
        name="pool_mix",
    )(ug, ug, ug, x, w_grp, scale, w_out, g, b)


def _tile(n, preferred):
    t = preferred
    while n % t:
        t //= 2
    return t


def kernel(x, l0_ffn1_w_in, l0_ffn1_w_out, l0_ln1_g, l0_ln1_b, l0_sb_w_qkv, l0_sb_w_o, l0_ln2_g, l0_ln2_b, l0_ffn2_w_in, l0_ffn2_w_out, l0_ln3_g, l0_ln3_b, l1_ffn1_w_in, l1_ffn1_w_out, l1_ln1_g, l1_ln1_b, l1_pool_w_in, l1_pool_w_grp, l1_pool_scale, l1_pool_w_out, l1_ln2_g, l1_ln2_b, l1_ffn2_w_in, l1_ffn2_w_out, l1_ln3_g, l1_ln3_b):
    bsz, seq, d = x.shape
    n = bsz * seq
    bf = lambda w: w.astype(BF16)
    vec = lambda p: p.reshape(1, -1).astype(F32)

    ffn = functools.partial(_ffn_ln, tm=_tile(n, 512), tf=_tile(l0_ffn1_w_out.shape[0], 512))
    h = x.reshape(n, d)

    h = ffn(h, bf(l0_ffn1_w_in), bf(l0_ffn1_w_out), vec(l0_ln1_g), vec(l0_ln1_b))
    qkv = _proj(h, bf(l0_sb_w_qkv), BF16, tm=_tile(n, 1024), tn=_tile(l0_sb_w_qkv.shape[1], 1024))
    o = _sb_attention(qkv, bsz=bsz, seq=seq, t_blk=_tile(seq, 256))
    h = _proj_res_ln(o, bf(l0_sb_w_o), h, vec(l0_ln2_g), vec(l0_ln2_b), tm=_tile(n, 512))
    h = ffn(h, bf(l0_ffn2_w_in), bf(l0_ffn2_w_out), vec(l0_ln3_g), vec(l0_ln3_b))

    h = ffn(h, bf(l1_ffn1_w_in), bf(l1_ffn1_w_out), vec(l1_ln1_g), vec(l1_ln1_b))
    ug = _proj(h, bf(l1_pool_w_in), F32, tm=_tile(n, 1024), tn=_tile(l1_pool_w_in.shape[1], 1024))
    h = _pool_mix(ug, h, bf(l1_pool_w_grp), vec(l1_pool_scale), bf(l1_pool_w_out), vec(l1_ln2_g), vec(l1_ln2_b),
                  seq=seq, tm=_tile(seq, 256))
    h = ffn(h, bf(l1_ffn2_w_in), bf(l1_ffn2_w_out), vec(l1_ln3_g), vec(l1_ln3_b))
    return h.reshape(bsz, seq, d)
```

```python
import functools
import math

import jax
import jax.numpy as jnp
from jax import lax
from jax.experimental import pallas as pl
from jax.experimental.pallas import tpu as pltpu

F32 = jnp.float32
BF16 = jnp.bfloat16

DEPTH = 2
DEEPNORM_ALPHA = (2 * DEPTH) ** 0.25
FFN_RESIDUAL = 0.5
LN_EPS = 1e-5
SB_HEADS = 8
SB_HEADS_PER_STEP = 2
POOL_WINDOWS = (2, 4, 8, 16)
POOL_HALO = 128

LOG2E = math.log2(math.e)
F32_EXP_ZERO_BELOW = -104.0
CUTOFF_MARGIN = 1.0
NO_KEYS = 1e30

VMEM_LIMIT = 56 * 1024 * 1024


def _params(semantics):
    return pltpu.CompilerParams(dimension_semantics=semantics, vmem_limit_bytes=VMEM_LIMIT)


def _layer_norm(y, g, b, eps=LN_EPS):
    mu = jnp.mean(y, axis=-1, keepdims=True)
    d = y - mu
    var = jnp.mean(d * d, axis=-1, keepdims=True)
    return d * lax.rsqrt(var + eps) * g + b


def _silu(a):
    return a * jax.nn.sigmoid(a)


def _dot(a, b):
    return jnp.dot(a, b, preferred_element_type=F32)


def _bits(x):
    return lax.bitcast_convert_type(x, jnp.uint32)


def _split_bf16(x):
    hi = lax.bitcast_convert_type(_bits(x) & jnp.uint32(0xFFFF0000), F32)
    return hi.astype(BF16), (x - hi).astype(BF16)


def _neg_abs(x):
    return lax.bitcast_convert_type(_bits(x) | jnp.uint32(0x80000000), F32)


def _ffn_ln_kernel(x_ref, wa_ref, wg_ref, wo_ref, g_ref, b_ref, o_ref, xb_ref, acc_ref, *, n_tiles):
    i = pl.program_id(0)
    j = pl.program_id(1)
    cur = i % 2
    rows = x_ref.shape[0] // pl.num_programs(1)

    def normalise_slice_of_previous_tile():
        r = pl.ds(pl.multiple_of(j * rows, rows), rows)
        a = acc_ref[1 - cur, r, :]
        o_ref[r, :] = _layer_norm(a, g_ref[...], b_ref[...], eps=LN_EPS / FFN_RESIDUAL ** 2)

    @pl.when(i < n_tiles)
    def _():
        @pl.when(j == 0)
        def _():
            x = x_ref[...]
            xb_ref[...] = x.astype(BF16)
            acc_ref[cur] = (DEEPNORM_ALPHA / FFN_RESIDUAL) * x

        @pl.when(jnp.logical_and(i == 0, j == 0))
        def _():
            acc_ref[1] = jnp.zeros(acc_ref.shape[1:], F32)

        normalise_slice_of_previous_tile()
        xb = xb_ref[...]
        a = _dot(xb, wa_ref[...])
        gate = _dot(xb, wg_ref[...])
        h = (_silu(a) * gate).astype(BF16)
        acc_ref[cur] += _dot(h, wo_ref[...])

    @pl.when(i == n_tiles)
    def _():
        normalise_slice_of_previous_tile()


def _ffn_ln(x, w_in, w_out, g, b, *, tm, tf):
    n, d = x.shape
    d_ff = w_out.shape[0]
    nj = d_ff // tf
    n_tiles = n // tm
    assert tm % nj == 0 and (tm // nj) % 8 == 0
    return pl.pallas_call(
        functools.partial(_ffn_ln_kernel, n_tiles=n_tiles),
        grid=(n_tiles + 1, nj),
        in_specs=[
            pl.BlockSpec((tm, d), lambda i, j: (jnp.minimum(i, n_tiles - 1), 0)),
            pl.BlockSpec((d, tf), lambda i, j: (0, j)),
            pl.BlockSpec((d, tf), lambda i, j: (0, nj + j)),
            pl.BlockSpec((tf, d), lambda i, j: (j, 0)),
            pl.BlockSpec((1, d), lambda i, j: (0, 0)),
            pl.BlockSpec((1, d), lambda i, j: (0, 0)),
        ],
        out_specs=pl.BlockSpec((tm, d), lambda i, j: (jnp.maximum(i - 1, 0), 0)),
        out_shape=jax.ShapeDtypeStruct((n, d), F32),
        scratch_shapes=[pltpu.VMEM((tm, d), BF16), pltpu.VMEM((2, tm, d), F32)],
        compiler_params=_params(("arbitrary", "arbitrary")),
        name="ffn_ln",
    )(x, w_in, w_in, w_out, g, b)


def _proj_kernel(x_ref, w_ref, o_ref, xb_ref):
    @pl.when(pl.program_id(1) == 0)
    def _():
        xb_ref[...] = x_ref[...].astype(BF16)

    o_ref[...] = _dot(xb_ref[...], w_ref[...]).astype(o_ref.dtype)


def _proj(x, w, out_dtype, *, tm, tn):
    n, d = x.shape
    n_out = w.shape[1]
    return pl.pallas_call(
        _proj_kernel,
        grid=(n // tm, n_out // tn),
        in_specs=[
            pl.BlockSpec((tm, d), lambda i, j: (i, 0)),
            pl.BlockSpec((d, tn), lambda i, j: (0, j)),
        ],
        out_specs=pl.BlockSpec((tm, tn), lambda i, j: (i, j)),
        out_shape=jax.ShapeDtypeStruct((n, n_out), out_dtype),
        scratch_shapes=[pltpu.VMEM((tm, d), BF16)],
        compiler_params=_params(("arbitrary", "arbitrary")),
        name="proj",
    )(x, w)


def _sb_attn_kernel(q_ref, k_ref, v_ref, o_ref, kmax2_ref, *, t_blk, r_blk, dh, scale):
    qi = pl.program_id(2)
    n_kblk = k_ref.shape[0] // t_blk
    heads = range(q_ref.shape[1] // dh)
    chunks = range(t_blk // r_blk)
    jobs = [(hh, ci) for hh in heads for ci in chunks]
    scale2 = scale * LOG2E
    cutoff2 = (F32_EXP_ZERO_BELOW - CUTOFF_MARGIN) * LOG2E

    def lanes(hh):
        return slice(hh * dh, (hh + 1) * dh)

    def rows(ci):
        return slice(ci * r_blk, (ci + 1) * r_blk)

    @pl.when(qi == 0)
    def _():
        def body(i, ms):
            k_rows = pl.ds(pl.multiple_of(i * t_blk, t_blk), t_blk)
            kfs = [k_ref[k_rows, lanes(hh)].astype(F32) for hh in heads]
            return tuple(jnp.maximum(m, jnp.sum(kf * kf, axis=-1, keepdims=True)) for m, kf in zip(ms, kfs))

        ms = lax.fori_loop(0, n_kblk, body, tuple(jnp.zeros((t_blk, 1), F32) for _ in heads))
        for hh in heads:
            kmax2_ref[hh] = jnp.max(ms[hh])

    def iota2(shape, axis):
        return lax.broadcasted_iota(jnp.int32, shape, axis)

    def suffix_ones(kw):
        return (iota2((kw, kw), 0) >= iota2((kw, kw), 1)).astype(BF16)

    ones_full = suffix_ones(t_blk)

    def z_bound2(hh, ci):
        qf = q_ref[rows(ci), lanes(hh)].astype(F32)
        return jnp.sqrt(jnp.sum(qf * qf, axis=-1, keepdims=True) * kmax2_ref[hh]) * scale2

    def key_width(ci, diagonal):
        return min(t_blk, -(-((ci + 1) * r_blk) // dh) * dh) if diagonal else t_blk

    def phase_scores(k_start, diagonal):
        ys = []
        for hh, ci in jobs:
            q = q_ref[rows(ci), lanes(hh)]
            k = k_ref[pl.ds(k_start, key_width(ci, diagonal)), lanes(hh)]
            ys.append(lax.dot_general(q, k, (((1,), (1,)), ((), ())), preferred_element_type=F32) * scale2)
        return ys

    def phase_tails(ys, diagonal):
        pasts, tails = [], []
        for (hh, ci), y in zip(jobs, ys):
            kw = key_width(ci, diagonal)
            sp = jnp.maximum(y, 0.0) + jnp.log2(1.0 + jnp.exp2(_neg_abs(y)))
            if diagonal:
                pasts.append(iota2((r_blk, kw), 1) < iota2((r_blk, kw), 0) + ci * r_blk)
                sp = jnp.where(pasts[-1], sp, 0.0)
            hi, lo = _split_bf16(sp)
            ones = ones_full if kw == t_blk else suffix_ones(kw)
            tails.append(_dot(hi, ones) + _dot(lo, ones))
        return tails, pasts

    def phase_output(k_start, ys, tails, pasts, cs, accs, diagonal):
        new_cs, new_accs = [], []
        for n, (hh, ci) in enumerate(jobs):
            w = jnp.exp2(jnp.minimum(ys[n] - tails[n] - cs[n], 0.0))
            if diagonal:
                w = jnp.where(pasts[n], w, 0.0)
            v = v_ref[pl.ds(k_start, key_width(ci, diagonal)), lanes(hh)]
            new_accs.append(accs[n] + _dot(w.astype(BF16), v))
            new_cs.append(cs[n] + tails[n][:, 0:1])
        return tuple(new_cs), tuple(new_accs)

    def live(cs):
        worst = functools.reduce(jnp.maximum, [z_bound2(hh, ci) - cs[n] for n, (hh, ci) in enumerate(jobs)])
        return jnp.max(worst) >= cutoff2

    diag_start = pl.multiple_of(qi * t_blk, t_blk)
    prev_start = pl.multiple_of(jnp.maximum(qi - 1, 0) * t_blk, t_blk)
    no_prev = jnp.where(qi == 0, jnp.float32(NO_KEYS), jnp.float32(0.0))
    y_d = phase_scores(diag_start, True)
    y_p = phase_scores(prev_start, False)
    tails_d, pasts_d = phase_tails(y_d, True)
    tails_p, _ = phase_tails(y_p, False)
    cs = tuple(jnp.zeros((r_blk, 1), F32) for _ in jobs)
    accs = tuple(jnp.zeros((r_blk, dh), F32) for _ in jobs)
    cs, accs = phase_output(diag_start, y_d, tails_d, pasts_d, cs, accs, True)
    cs = tuple(c + no_prev for c in cs)
    cs, accs = phase_output(prev_start, y_p, tails_p, None, cs, accs, False)

    def cond(carry):
        kb, go, _, _ = carry
        return jnp.logical_and(kb >= 0, go)

    def body(carry):
        kb, _, cs, accs = carry
        k_start = pl.multiple_of(kb * t_blk, t_blk)
        ys = phase_scores(k_start, False)
        tails, _ = phase_tails(ys, False)
        cs, accs = phase_output(k_start, ys, tails, None, cs, accs, False)
        return kb - 1, live(cs), cs, accs

    _, _, _, accs = lax.while_loop(cond, body, (qi - 2, live(cs), cs, accs))
    for n, (hh, ci) in enumerate(jobs):
        o_ref[rows(ci), lanes(hh)] = accs[n].astype(o_ref.dtype)


def _sb_attention(qkv, *, bsz, seq, t_blk, r_blk):
    n, three_inner = qkv.shape
    inner = three_inner // 3
    dh = inner // SB_HEADS
    hp = SB_HEADS_PER_STEP
    groups = SB_HEADS // hp
    nq = seq // t_blk
    return pl.pallas_call(
        functools.partial(_sb_attn_kernel, t_blk=t_blk, r_blk=r_blk, dh=dh, scale=dh ** -0.5),
        grid=(bsz, groups, nq),
        in_specs=[
            pl.BlockSpec((t_blk, hp * dh), lambda b, h, i: (b * nq + i, h)),
            pl.BlockSpec((seq, hp * dh), lambda b, h, i: (b, groups + h)),
            pl.BlockSpec((seq, hp * dh), lambda b, h, i: (b, 2 * groups + h)),
        ],
        out_specs=pl.BlockSpec((t_blk, hp * dh), lambda b, h, i: (b * nq + i, h)),
        out_shape=jax.ShapeDtypeStruct((n, inner), BF16),
        scratch_shapes=[pltpu.SMEM((hp,), F32)],
        compiler_params=_params(("arbitrary", "arbitrary", "arbitrary")),
        name="sb_attn",
    )(qkv, qkv, qkv)


def _proj_res_ln_kernel(o_ref, w_ref, x_ref, g_ref, b_ref, out_ref):
    y = DEEPNORM_ALPHA * x_ref[...] + _dot(o_ref[...], w_ref[...])
    out_ref[...] = _layer_norm(y, g_ref[...], b_ref[...])


def _proj_res_ln(o, w, x, g, b, *, tm):
    n, d = x.shape
    inner = o.shape[1]
    return pl.pallas_call(
        _proj_res_ln_kernel,
        grid=(n // tm,),
        in_specs=[
            pl.BlockSpec((tm, inner), lambda i: (i, 0)),
            pl.BlockSpec((inner, d), lambda i: (0, 0)),
            pl.BlockSpec((tm, d), lambda i: (i, 0)),
            pl.BlockSpec((1, d), lambda i: (0, 0)),
            pl.BlockSpec((1, d), lambda i: (0, 0)),
        ],
        out_specs=pl.BlockSpec((tm, d), lambda i: (i, 0)),
        out_shape=jax.ShapeDtypeStruct((n, d), F32),
        compiler_params=_params(("arbitrary",)),
        name="proj_res_ln",
    )(o, w, x, g, b)


def _pool_mix_kernel(u_ref, halo_ref, gate_ref, x_ref, wgrp_ref, scale_ref, wout_ref, g_ref, b_ref,
                     out_ref, t_ref, *, tiles_per_seq):
    tm, d = x_ref.shape
    cg = d // len(POOL_WINDOWS)
    tile_in_seq = pl.program_id(0) % tiles_per_seq

    r = lax.broadcasted_iota(jnp.int32, (tm, tm + POOL_HALO), 0) + POOL_HALO
    c = lax.broadcasted_iota(jnp.int32, (tm, tm + POOL_HALO), 1)
    pos = tile_in_seq * tm + lax.broadcasted_iota(jnp.int32, (tm, 1), 0)
    halo_in_seq = tile_in_seq * tm - POOL_HALO + lax.broadcasted_iota(jnp.int32, (POOL_HALO, 1), 0) >= 0

    for gi, window in enumerate(POOL_WINDOWS):
        cols = slice(gi * cg, (gi + 1) * cg)
        u = u_ref[:, cols]
        halo = jnp.where(halo_in_seq, halo_ref[:, cols], 0.0)
        hi, lo = _split_bf16(jnp.concatenate([halo, u], axis=0))
        band = jnp.logical_and(c > r - window, c <= r).astype(BF16)
        wsum = _dot(band, hi) + _dot(band, lo)
        count = jnp.minimum(pos + 1, window).astype(F32)
        pooled = wsum / count - u
        mixed = _dot(pooled.astype(BF16), wgrp_ref[gi])
        t_ref[:, cols] = (scale_ref[:, cols] * mixed * _silu(gate_ref[:, cols])).astype(BF16)

    y = DEEPNORM_ALPHA * x_ref[...] + _dot(t_ref[...], wout_ref[...])
    out_ref[...] = _layer_norm(y, g_ref[...], b_ref[...])


def _pool_mix(ug, x, w_grp, scale, w_out, g, b, *, seq, tm):
    n, d = x.shape
    groups, cg, _ = w_grp.shape
    halo_per_tile = tm // POOL_HALO
    return pl.pallas_call(
        functools.partial(_pool_mix_kernel, tiles_per_seq=seq // tm),
        grid=(n // tm,),
        in_specs=[
            pl.BlockSpec((tm, d), lambda i: (i, 0)),
            pl.BlockSpec((POOL_HALO, d), lambda i: (jnp.maximum(i * halo_per_tile - 1, 0), 0)),
            pl.BlockSpec((tm, d), lambda i: (i, 1)),
            pl.BlockSpec((tm, d), lambda i: (i, 0)),
            pl.BlockSpec((groups, cg, cg), lambda i: (0, 0, 0)),
            pl.BlockSpec((1, d), lambda i: (0, 0)),
            pl.BlockSpec((d, d), lambda i: (0, 0)),
            pl.BlockSpec((1, d), lambda i: (0, 0)),
            pl.BlockSpec((1, d), lambda i: (0, 0)),
        ],
        out_specs=pl.BlockSpec((tm, d), lambda i: (i, 0)),
        out_shape=jax.ShapeDtypeStruct((n, d), F32),
        scratch_shapes=[pltpu.VMEM((tm, d), BF16)],
        compiler_params=_params(("arbitrary",)),
        name="pool_mix",
    )(ug, ug, ug, x, w_grp, scale, w_out, g, b)


def _tile(n, preferred):
    t = preferred
    while n % t:
        t //= 2
    return t


def kernel(x, l0_ffn1_w_in, l0_ffn1_w_out, l0_ln1_g, l0_ln1_b, l0_sb_w_qkv, l0_sb_w_o, l0_ln2_g, l0_ln2_b, l0_ffn2_w_in, l0_ffn2_w_out, l0_ln3_g, l0_ln3_b, l1_ffn1_w_in, l1_ffn1_w_out, l1_ln1_g, l1_ln1_b, l1_pool_w_in, l1_pool_w_grp, l1_pool_scale, l1_pool_w_out, l1_ln2_g, l1_ln2_b, l1_ffn2_w_in, l1_ffn2_w_out, l1_ln3_g, l1_ln3_b):
    bsz, seq, d = x.shape
    n = bsz * seq
    bf = lambda w: w.astype(BF16)
    vec = lambda p: p.reshape(1, -1).astype(F32)

    ffn = functools.partial(_ffn_ln, tm=_tile(n, 512), tf=_tile(l0_ffn1_w_out.shape[0], 512))
    h = x.reshape(n, d)

    h = ffn(h, bf(l0_ffn1_w_in), bf(l0_ffn1_w_out), vec(l0_ln1_g), vec(l0_ln1_b))
    qkv = _proj(h, bf(l0_sb_w_qkv), BF16, tm=_tile(n, 1024), tn=_tile(l0_sb_w_qkv.shape[1], 1024))
    o = _sb_attention(qkv, bsz=bsz, seq=seq, t_blk=_tile(seq, 256), r_blk=128)
    h = _proj_res_ln(o, bf(l0_sb_w_o), h, vec(l0_ln2_g), vec(l0_ln2_b), tm=_tile(n, 512))
    h = ffn(h, bf(l0_ffn2_w_in), bf(l0_ffn2_w_out), vec(l0_ln3_g), vec(l0_ln3_b))

    h = ffn(h, bf(l1_ffn1_w_in), bf(l1_ffn1_w_out), vec(l1_ln1_g), vec(l1_ln1_b))
    ug = _proj(h, bf(l1_pool_w_in), F32, tm=_tile(n, 1024), tn=_tile(l1_pool_w_in.shape[1], 1024))
    h = _pool_mix(ug, h, bf(l1_pool_w_grp), vec(l1_pool_scale), bf(l1_pool_w_out), vec(l1_ln2_g), vec(l1_ln2_b),
                  seq=seq, tm=_tile(seq, 256))
    h = ffn(h, bf(l1_ffn2_w_in), bf(l1_ffn2_w_out), vec(l1_ln3_g), vec(l1_ln3_b))
    return h.reshape(bsz, seq, d)
```

```python
import functools
import math

import jax
import jax.numpy as jnp
from jax import lax
from jax.experimental import pallas as pl
from jax.experimental.pallas import tpu as pltpu

F32 = jnp.float32
BF16 = jnp.bfloat16

DEPTH = 2
DEEPNORM_ALPHA = (2 * DEPTH) ** 0.25
FFN_RESIDUAL = 0.5
LN_EPS = 1e-5
SB_HEADS = 8
SB_HEADS_PER_STEP = 2
POOL_WINDOWS = (2, 4, 8, 16)
POOL_HALO = max(POOL_WINDOWS)

LOG2E = math.log2(math.e)
F32_EXP_ZERO_BELOW = -104.0
CUTOFF_MARGIN = 1.0
NO_KEYS = 1e30

VMEM_LIMIT = 56 * 1024 * 1024


def _params(semantics):
    return pltpu.CompilerParams(dimension_semantics=semantics, vmem_limit_bytes=VMEM_LIMIT)


def _layer_norm(y, g, b, eps=LN_EPS):
    mu = jnp.mean(y, axis=-1, keepdims=True)
    d = y - mu
    var = jnp.mean(d * d, axis=-1, keepdims=True)
    return d * lax.rsqrt(var + eps) * g + b


def _silu(a):
    return a * jax.nn.sigmoid(a)


def _dot(a, b):
    return jnp.dot(a, b, preferred_element_type=F32)


def _bits(x):
    return lax.bitcast_convert_type(x, jnp.uint32)


def _split_bf16(x):
    hi = lax.bitcast_convert_type(_bits(x) & jnp.uint32(0xFFFF0000), F32)
    return hi.astype(BF16), (x - hi).astype(BF16)


def _neg_abs(x):
    return lax.bitcast_convert_type(_bits(x) | jnp.uint32(0x80000000), F32)


def _ffn_ln_kernel(x_ref, wa_ref, wg_ref, wo_ref, g_ref, b_ref, o_ref, xb_ref, acc_ref, *, n_tiles):
    i = pl.program_id(0)
    j = pl.program_id(1)
    cur = i % 2
    rows = x_ref.shape[0] // pl.num_programs(1)

    def normalise_slice_of_previous_tile():
        a = acc_ref[1 - cur, pl.ds(pl.multiple_of(j * rows, rows), rows), :]
        o_ref[...] = _layer_norm(a, g_ref[...], b_ref[...], eps=LN_EPS / FFN_RESIDUAL ** 2)

    @pl.when(i < n_tiles)
    def _():
        @pl.when(j == 0)
        def _():
            x = x_ref[...]
            xb_ref[...] = x.astype(BF16)
            acc_ref[cur] = (DEEPNORM_ALPHA / FFN_RESIDUAL) * x

        @pl.when(jnp.logical_and(i == 0, j == 0))
        def _():
            acc_ref[1] = jnp.zeros(acc_ref.shape[1:], F32)

        normalise_slice_of_previous_tile()
        xb = xb_ref[...]
        a = _dot(xb, wa_ref[...])
        gate = _dot(xb, wg_ref[...])
        h = (_silu(a) * gate).astype(BF16)
        acc_ref[cur] += _dot(h, wo_ref[...])

    @pl.when(i == n_tiles)
    def _():
        normalise_slice_of_previous_tile()


def _ffn_ln(x, w_in, w_out, g, b, *, tm, tf):
    n, d = x.shape
    d_ff = w_out.shape[0]
    nj = d_ff // tf
    n_tiles = n // tm
    assert tm % nj == 0 and (tm // nj) % 8 == 0
    return pl.pallas_call(
        functools.partial(_ffn_ln_kernel, n_tiles=n_tiles),
        grid=(n_tiles + 1, nj),
        in_specs=[
            pl.BlockSpec((tm, d), lambda i, j: (jnp.minimum(i, n_tiles - 1), 0)),
            pl.BlockSpec((d, tf), lambda i, j: (0, j)),
            pl.BlockSpec((d, tf), lambda i, j: (0, nj + j)),
            pl.BlockSpec((tf, d), lambda i, j: (j, 0)),
            pl.BlockSpec((1, d), lambda i, j: (0, 0)),
            pl.BlockSpec((1, d), lambda i, j: (0, 0)),
        ],
        out_specs=pl.BlockSpec((tm // nj, d), lambda i, j: (jnp.maximum((i - 1) * nj + j, 0), 0)),
        out_shape=jax.ShapeDtypeStruct((n, d), F32),
        scratch_shapes=[pltpu.VMEM((tm, d), BF16), pltpu.VMEM((2, tm, d), F32)],
        compiler_params=_params(("arbitrary", "arbitrary")),
        name="ffn_ln",
    )(x, w_in, w_in, w_out, g, b)


def _proj_kernel(x_ref, w_ref, o_ref, xb_ref):
    @pl.when(pl.program_id(1) == 0)
    def _():
        xb_ref[...] = x_ref[...].astype(BF16)

    o_ref[...] = _dot(xb_ref[...], w_ref[...]).astype(o_ref.dtype)


def _proj(x, w, out_dtype, *, tm, tn):
    n, d = x.shape
    n_out = w.shape[1]
    return pl.pallas_call(
        _proj_kernel,
        grid=(n // tm, n_out // tn),
        in_specs=[
            pl.BlockSpec((tm, d), lambda i, j: (i, 0)),
            pl.BlockSpec((d, tn), lambda i, j: (0, j)),
        ],
        out_specs=pl.BlockSpec((tm, tn), lambda i, j: (i, j)),
        out_shape=jax.ShapeDtypeStruct((n, n_out), out_dtype),
        scratch_shapes=[pltpu.VMEM((tm, d), BF16)],
        compiler_params=_params(("arbitrary", "arbitrary")),
        name="proj",
    )(x, w)


def _sb_attn_kernel(q_ref, k_ref, v_ref, o_ref, kmax2_ref, *, t_blk, r_blk, dh, scale):
    qi = pl.program_id(2)
    n_kblk = k_ref.shape[0] // t_blk
    heads = range(q_ref.shape[1] // dh)
    chunks = range(t_blk // r_blk)
    jobs = [(hh, ci) for hh in heads for ci in chunks]
    scale2 = scale * LOG2E
    cutoff2 = (F32_EXP_ZERO_BELOW - CUTOFF_MARGIN) * LOG2E

    def lanes(hh):
        return slice(hh * dh, (hh + 1) * dh)

    def rows(ci):
        return slice(ci * r_blk, (ci + 1) * r_blk)

    @pl.when(qi == 0)
    def _():
        def body(i, ms):
            k_rows = pl.ds(pl.multiple_of(i * t_blk, t_blk), t_blk)
            kfs = [k_ref[k_rows, lanes(hh)].astype(F32) for hh in heads]
            return tuple(jnp.maximum(m, jnp.sum(kf * kf, axis=-1, keepdims=True)) for m, kf in zip(ms, kfs))

        ms = lax.fori_loop(0, n_kblk, body, tuple(jnp.zeros((t_blk, 1), F32) for _ in heads))
        for hh in heads:
            kmax2_ref[hh] = jnp.max(ms[hh])

    def iota2(shape, axis):
        return lax.broadcasted_iota(jnp.int32, shape, axis)

    def suffix_ones(kw):
        return (iota2((kw, kw), 0) >= iota2((kw, kw), 1)).astype(BF16)

    ones_full = suffix_ones(t_blk)

    def z_bound2(hh, ci):
        qf = q_ref[rows(ci), lanes(hh)].astype(F32)
        return jnp.sqrt(jnp.sum(qf * qf, axis=-1, keepdims=True) * kmax2_ref[hh]) * scale2

    def key_width(ci, diagonal):
        return min(t_blk, -(-((ci + 1) * r_blk) // dh) * dh) if diagonal else t_blk

    def phase_scores(k_start, diagonal):
        ys = []
        for hh, ci in jobs:
            q = q_ref[rows(ci), lanes(hh)]
            k = k_ref[pl.ds(k_start, key_width(ci, diagonal)), lanes(hh)]
            ys.append(lax.dot_general(q, k, (((1,), (1,)), ((), ())), preferred_element_type=F32) * scale2)
        return ys

    def phase_tails(ys, diagonal):
        pasts, tails = [], []
        for (hh, ci), y in zip(jobs, ys):
            kw = key_width(ci, diagonal)
            sp = jnp.maximum(y, 0.0) + jnp.log2(1.0 + jnp.exp2(_neg_abs(y)))
            if diagonal:
                pasts.append(iota2((r_blk, kw), 1) < iota2((r_blk, kw), 0) + ci * r_blk)
                sp = jnp.where(pasts[-1], sp, 0.0)
            hi, lo = _split_bf16(sp)
            ones = ones_full if kw == t_blk else suffix_ones(kw)
            tails.append(_dot(hi, ones) + _dot(lo, ones))
        return tails, pasts

    def phase_output(k_start, ys, tails, pasts, cs, accs, diagonal):
        new_cs, new_accs = [], []
        for n, (hh, ci) in enumerate(jobs):
            w = jnp.exp2(jnp.minimum(ys[n] - tails[n] - cs[n], 0.0))
            if diagonal:
                w = jnp.where(pasts[n], w, 0.0)
            v = v_ref[pl.ds(k_start, key_width(ci, diagonal)), lanes(hh)]
            new_accs.append(accs[n] + _dot(w.astype(BF16), v))
            new_cs.append(cs[n] + tails[n][:, 0:1])
        return tuple(new_cs), tuple(new_accs)

    def live(cs):
        worst = functools.reduce(jnp.maximum, [z_bound2(hh, ci) - cs[n] for n, (hh, ci) in enumerate(jobs)])
        return jnp.max(worst) >= cutoff2

    diag_start = pl.multiple_of(qi * t_blk, t_blk)
    prev_start = pl.multiple_of(jnp.maximum(qi - 1, 0) * t_blk, t_blk)
    no_prev = jnp.where(qi == 0, jnp.float32(NO_KEYS), jnp.float32(0.0))
    y_d = phase_scores(diag_start, True)
    y_p = phase_scores(prev_start, False)
    tails_d, pasts_d = phase_tails(y_d, True)
    tails_p, _ = phase_tails(y_p, False)
    cs = tuple(jnp.zeros((r_blk, 1), F32) for _ in jobs)
    accs = tuple(jnp.zeros((r_blk, dh), F32) for _ in jobs)
    cs, accs = phase_output(diag_start, y_d, tails_d, pasts_d, cs, accs, True)
    cs = tuple(c + no_prev for c in cs)
    cs, accs = phase_output(prev_start, y_p, tails_p, None, cs, accs, False)

    def cond(carry):
        kb, go, _, _ = carry
        return jnp.logical_and(kb >= 0, go)

    def body(carry):
        kb, _, cs, accs = carry
        k_start = pl.multiple_of(kb * t_blk, t_blk)
        ys = phase_scores(k_start, False)
        tails, _ = phase_tails(ys, False)
        cs, accs = phase_output(k_start, ys, tails, None, cs, accs, False)
        return kb - 1, live(cs), cs, accs

    _, _, _, accs = lax.while_loop(cond, body, (qi - 2, live(cs), cs, accs))
    for n, (hh, ci) in enumerate(jobs):
        o_ref[rows(ci), lanes(hh)] = accs[n].astype(o_ref.dtype)


def _sb_attention(qkv, *, bsz, seq, t_blk, r_blk):
    n, three_inner = qkv.shape
    inner = three_inner // 3
    dh = inner // SB_HEADS
    hp = SB_HEADS_PER_STEP
    groups = SB_HEADS // hp
    nq = seq // t_blk
    return pl.pallas_call(
        functools.partial(_sb_attn_kernel, t_blk=t_blk, r_blk=r_blk, dh=dh, scale=dh ** -0.5),
        grid=(bsz, groups, nq),
        in_specs=[
            pl.BlockSpec((t_blk, hp * dh), lambda b, h, i: (b * nq + i, h)),
            pl.BlockSpec((seq, hp * dh), lambda b, h, i: (b, groups + h)),
            pl.BlockSpec((seq, hp * dh), lambda b, h, i: (b, 2 * groups + h)),
        ],
        out_specs=pl.BlockSpec((t_blk, hp * dh), lambda b, h, i: (b * nq + i, h)),
        out_shape=jax.ShapeDtypeStruct((n, inner), BF16),
        scratch_shapes=[pltpu.SMEM((hp,), F32)],
        compiler_params=_params(("arbitrary", "arbitrary", "arbitrary")),
        name="sb_attn",
    )(qkv, qkv, qkv)


def _proj_res_ln_kernel(o_ref, w_ref, x_ref, g_ref, b_ref, out_ref):
    y = DEEPNORM_ALPHA * x_ref[...] + _dot(o_ref[...], w_ref[...])
    out_ref[...] = _layer_norm(y, g_ref[...], b_ref[...])


def _proj_res_ln(o, w, x, g, b, *, tm):
    n, d = x.shape
    inner = o.shape[1]
    return pl.pallas_call(
        _proj_res_ln_kernel,
        grid=(n // tm,),
        in_specs=[
            pl.BlockSpec((tm, inner), lambda i: (i, 0)),
            pl.BlockSpec((inner, d), lambda i: (0, 0)),
            pl.BlockSpec((tm, d), lambda i: (i, 0)),
            pl.BlockSpec((1, d), lambda i: (0, 0)),
            pl.BlockSpec((1, d), lambda i: (0, 0)),
        ],
        out_specs=pl.BlockSpec((tm, d), lambda i: (i, 0)),
        out_shape=jax.ShapeDtypeStruct((n, d), F32),
        compiler_params=_params(("arbitrary",)),
        name="proj_res_ln",
    )(o, w, x, g, b)


def _pool_mix_kernel(u_ref, halo_ref, gate_ref, x_ref, wgrp_ref, scale_ref, wout_ref, g_ref, b_ref,
                     out_ref, t_ref, *, tiles_per_seq):
    tm, d = x_ref.shape
    cg = d // len(POOL_WINDOWS)
    tile_in_seq = pl.program_id(0) % tiles_per_seq

    pos = tile_in_seq * tm + lax.broadcasted_iota(jnp.int32, (tm, 1), 0)
    halo_in_seq = tile_in_seq * tm - POOL_HALO + lax.broadcasted_iota(jnp.int32, (POOL_HALO, 1), 0) >= 0

    for gi, window in enumerate(POOL_WINDOWS):
        cols = slice(gi * cg, (gi + 1) * cg)
        u = u_ref[:, cols]
        halo = jnp.where(halo_in_seq, halo_ref[:, cols], 0.0)
        s = jnp.concatenate([halo, u], axis=0)
        span = 1
        while span < window:
            s = s + pltpu.roll(s, span, axis=0)
            span *= 2
        count = jnp.minimum(pos + 1, window).astype(F32)
        pooled = s[POOL_HALO:] / count - u
        mixed = _dot(pooled.astype(BF16), wgrp_ref[gi])
        t_ref[:, cols] = (scale_ref[:, cols] * mixed * _silu(gate_ref[:, cols])).astype(BF16)

    y = DEEPNORM_ALPHA * x_ref[...] + _dot(t_ref[...], wout_ref[...])
    out_ref[...] = _layer_norm(y, g_ref[...], b_ref[...])


def _pool_mix(ug, x, w_grp, scale, w_out, g, b, *, seq, tm):
    n, d = x.shape
    groups, cg, _ = w_grp.shape
    halo_per_tile = tm // POOL_HALO
    return pl.pallas_call(
        functools.partial(_pool_mix_kernel, tiles_per_seq=seq // tm),
        grid=(n // tm,),
        in_specs=[
            pl.BlockSpec((tm, d), lambda i: (i, 0)),
            pl.BlockSpec((POOL_HALO, d), lambda i: (jnp.maximum(i * halo_per_tile - 1, 0), 0)),
            pl.BlockSpec((tm, d), lambda i: (i, 1)),
            pl.BlockSpec((tm, d), lambda i: (i, 0)),
            pl.BlockSpec((groups, cg, cg), lambda i: (0, 0, 0)),
            pl.BlockSpec((1, d), lambda i: (0, 0)),
            pl.BlockSpec((d, d), lambda i: (0, 0)),
            pl.BlockSpec((1, d), lambda i: (0, 0)),
            pl.BlockSpec((1, d), lambda i: (0, 0)),
        ],
        out_specs=pl.BlockSpec((tm, d), lambda i: (i, 0)),
        out_shape=jax.ShapeDtypeStruct((n, d), F32),
        scratch_shapes=[pltpu.VMEM((tm, d), BF16)],
        compiler_params=_params(("arbitrary",)),
        name="pool_mix",
    )(ug, ug, ug, x, w_grp, scale, w_out, g, b)


def _tile(n, preferred):
    t = preferred
    while n % t:
        t //= 2
    return t


def kernel(x, l0_ffn1_w_in, l0_ffn1_w_out, l0_ln1_g, l0_ln1_b, l0_sb_w_qkv, l0_sb_w_o, l0_ln2_g, l0_ln2_b, l0_ffn2_w_in, l0_ffn2_w_out, l0_ln3_g, l0_ln3_b, l1_ffn1_w_in, l1_ffn1_w_out, l1_ln1_g, l1_ln1_b, l1_pool_w_in, l1_pool_w_grp, l1_pool_scale, l1_pool_w_out, l1_ln2_g, l1_ln2_b, l1_ffn2_w_in, l1_ffn2_w_out, l1_ln3_g, l1_ln3_b):
    bsz, seq, d = x.shape
    n = bsz * seq
    bf = lambda w: w.astype(BF16)
    vec = lambda p: p.reshape(1, -1).astype(F32)

    ffn = functools.partial(_ffn_ln, tm=_tile(n, 512), tf=_tile(l0_ffn1_w_out.shape[0], 1024))
    h = x.reshape(n, d)

    h = ffn(h, bf(l0_ffn1_w_in), bf(l0_ffn1_w_out), vec(l0_ln1_g), vec(l0_ln1_b))
    qkv = _proj(h, bf(l0_sb_w_qkv), BF16, tm=_tile(n, 1024), tn=_tile(l0_sb_w_qkv.shape[1], 1024))
    o = _sb_attention(qkv, bsz=bsz, seq=seq, t_blk=_tile(seq, 256), r_blk=128)
    h = _proj_res_ln(o, bf(l0_sb_w_o), h, vec(l0_ln2_g), vec(l0_ln2_b), tm=_tile(n, 512))
    h = ffn(h, bf(l0_ffn2_w_in), bf(l0_ffn2_w_out), vec(l0_ln3_g), vec(l0_ln3_b))

    h = ffn(h, bf(l1_ffn1_w_in), bf(l1_ffn1_w_out), vec(l1_ln1_g), vec(l1_ln1_b))
    ug = _proj(h, bf(l1_pool_w_in), F32, tm=_tile(n, 1024), tn=_tile(l1_pool_w_in.shape[1], 1024))
    h = _pool_mix(ug, h, bf(l1_pool_w_grp), vec(l1_pool_scale), bf(l1_pool_w_out), vec(l1_ln2_g), vec(l1_ln2_b),
                  seq=seq, tm=_tile(seq, 256))
    h = ffn(h, bf(l1_ffn2_w_in), bf(l1_ffn2_w_out), vec(l1_ln3_g), vec(l1_ln3_b))
    return h.reshape(bsz, seq, d)
```

```python
import functools
import math

import jax
import jax.numpy as jnp
from jax import lax
from jax.experimental import pallas as pl
from jax.experimental.pallas import tpu as pltpu

F32 = jnp.float32
BF16 = jnp.bfloat16

DEPTH = 2
DEEPNORM_ALPHA = (2 * DEPTH) ** 0.25
FFN_RESIDUAL = 0.5
LN_EPS = 1e-5
SB_HEADS = 8
SB_HEADS_PER_STEP = 2
POOL_WINDOWS = (2, 4, 8, 16)
POOL_HALO = max(POOL_WINDOWS)

LOG2E = math.log2(math.e)
F32_EXP_ZERO_BELOW = -104.0
CUTOFF_MARGIN = 1.0
NO_KEYS = 1e30
NORM_SLACK = 1.01

VMEM_LIMIT = 56 * 1024 * 1024


def _params(semantics):
    return pltpu.CompilerParams(dimension_semantics=semantics, vmem_limit_bytes=VMEM_LIMIT)


def _layer_norm(y, g, b, eps=LN_EPS):
    mu = jnp.mean(y, axis=-1, keepdims=True)
    d = y - mu
    var = jnp.mean(d * d, axis=-1, keepdims=True)
    return d * lax.rsqrt(var + eps) * g + b


def _silu(a):
    return a * jax.nn.sigmoid(a)


def _dot(a, b):
    return jnp.dot(a, b, preferred_element_type=F32)


def _bits(x):
    return lax.bitcast_convert_type(x, jnp.uint32)


def _split_bf16(x):
    hi = lax.bitcast_convert_type(_bits(x) & jnp.uint32(0xFFFF0000), F32)
    return hi.astype(BF16), (x - hi).astype(BF16)


def _neg_abs(x):
    return lax.bitcast_convert_type(_bits(x) | jnp.uint32(0x80000000), F32)


def _ffn_ln_kernel(x_ref, wa_ref, wg_ref, wo_ref, g_ref, b_ref, o_ref, xb_ref, acc_ref, *, n_tiles):
    i = pl.program_id(0)
    j = pl.program_id(1)
    cur = i % 2
    rows = x_ref.shape[0] // pl.num_programs(1)

    def normalise_slice_of_previous_tile():
        a = acc_ref[1 - cur, pl.ds(pl.multiple_of(j * rows, rows), rows), :]
        o_ref[...] = _layer_norm(a, g_ref[...], b_ref[...], eps=LN_EPS / FFN_RESIDUAL ** 2)

    @pl.when(i < n_tiles)
    def _():
        @pl.when(j == 0)
        def _():
            x = x_ref[...]
            xb_ref[...] = x.astype(BF16)
            acc_ref[cur] = (DEEPNORM_ALPHA / FFN_RESIDUAL) * x

        @pl.when(jnp.logical_and(i == 0, j == 0))
        def _():
            acc_ref[1] = jnp.zeros(acc_ref.shape[1:], F32)

        xb = xb_ref[...]
        a = _dot(xb, wa_ref[...])
        normalise_slice_of_previous_tile()
        gate = _dot(xb, wg_ref[...])
        h = (_silu(a) * gate).astype(BF16)
        acc_ref[cur] += _dot(h, wo_ref[...])

    @pl.when(i == n_tiles)
    def _():
        normalise_slice_of_previous_tile()


def _ffn_ln(x, w_in, w_out, g, b, *, tm, tf):
    n, d = x.shape
    d_ff = w_out.shape[0]
    nj = d_ff // tf
    n_tiles = n // tm
    assert tm % nj == 0 and (tm // nj) % 8 == 0
    return pl.pallas_call(
        functools.partial(_ffn_ln_kernel, n_tiles=n_tiles),
        grid=(n_tiles + 1, nj),
        in_specs=[
            pl.BlockSpec((tm, d), lambda i, j: (jnp.minimum(i, n_tiles - 1), 0)),
            pl.BlockSpec((d, tf), lambda i, j: (0, j)),
            pl.BlockSpec((d, tf), lambda i, j: (0, nj + j)),
            pl.BlockSpec((tf, d), lambda i, j: (j, 0)),
            pl.BlockSpec((1, d), lambda i, j: (0, 0)),
            pl.BlockSpec((1, d), lambda i, j: (0, 0)),
        ],
        out_specs=pl.BlockSpec((tm // nj, d), lambda i, j: (jnp.maximum((i - 1) * nj + j, 0), 0)),
        out_shape=jax.ShapeDtypeStruct((n, d), F32),
        scratch_shapes=[pltpu.VMEM((tm, d), BF16), pltpu.VMEM((2, tm, d), F32)],
        compiler_params=_params(("arbitrary", "arbitrary")),
        name="ffn_ln",
    )(x, w_in, w_in, w_out, g, b)


def _proj_kernel(x_ref, w_ref, o_ref, xb_ref):
    @pl.when(pl.program_id(1) == 0)
    def _():
        xb_ref[...] = x_ref[...].astype(BF16)

    o_ref[...] = _dot(xb_ref[...], w_ref[...]).astype(o_ref.dtype)


def _proj(x, w, out_dtype, *, tm, tn):
    n, d = x.shape
    n_out = w.shape[1]
    return pl.pallas_call(
        _proj_kernel,
        grid=(n // tm, n_out // tn),
        in_specs=[
            pl.BlockSpec((tm, d), lambda i, j: (i, 0)),
            pl.BlockSpec((d, tn), lambda i, j: (0, j)),
        ],
        out_specs=pl.BlockSpec((tm, tn), lambda i, j: (i, j)),
        out_shape=jax.ShapeDtypeStruct((n, n_out), out_dtype),
        scratch_shapes=[pltpu.VMEM((tm, d), BF16)],
        compiler_params=_params(("arbitrary", "arbitrary")),
        name="proj",
    )(x, w)


def _sb_attn_kernel(q_ref, k_ref, v_ref, o_ref, kmax2_ref, *, t_blk, r_blk, dh, scale):
    qi = pl.program_id(2)
    n_kblk = k_ref.shape[0] // t_blk
    n_sub = q_ref.shape[0] // t_blk
    heads = range(q_ref.shape[1] // dh)
    chunks = range(t_blk // r_blk)
    jobs = [(s, hh, ci) for s in range(n_sub) for hh in heads for ci in chunks]
    scale2 = scale * LOG2E
    cutoff2 = (F32_EXP_ZERO_BELOW - CUTOFF_MARGIN) * LOG2E

    def lanes(hh):
        return slice(hh * dh, (hh + 1) * dh)

    def rows(s, ci):
        r0 = s * t_blk + ci * r_blk
        return slice(r0, r0 + r_blk)

    def key_tile(s, p):
        return qi * n_sub + s - p

    def key_start(s, p):
        return pl.multiple_of(jnp.maximum(key_tile(s, p), 0) * t_blk, t_blk)

    def out_of_keys(s, p):
        return jnp.where(key_tile(s, p) < 0, jnp.float32(NO_KEYS), jnp.float32(0.0))

    @pl.when(qi == 0)
    def _():
        ones = jnp.ones((dh, dh), BF16)

        def body(i, ms):
            k_rows = pl.ds(pl.multiple_of(i * t_blk, t_blk), t_blk)
            ks = [k_ref[k_rows, lanes(hh)] for hh in heads]
            return tuple(jnp.maximum(m, _dot(k * k, ones)) for m, k in zip(ms, ks))

        ms = lax.fori_loop(0, n_kblk, body, tuple(jnp.zeros((t_blk, dh), F32) for _ in heads),
                           unroll=math.gcd(n_kblk, 8))
        for hh in heads:
            kmax2_ref[hh] = jnp.max(ms[hh]) * NORM_SLACK

    def iota2(shape, axis):
        return lax.broadcasted_iota(jnp.int32, shape, axis)

    def suffix_ones(kw):
        return (iota2((kw, kw), 0) >= iota2((kw, kw), 1)).astype(BF16)

    ones_full = suffix_ones(t_blk)

    def z_bound2(s, hh, ci):
        qf = q_ref[rows(s, ci), lanes(hh)].astype(F32)
        return jnp.sqrt(jnp.sum(qf * qf, axis=-1, keepdims=True) * kmax2_ref[hh]) * scale2

    def key_width(ci, diagonal):
        return min(t_blk, -(-((ci + 1) * r_blk) // dh) * dh) if diagonal else t_blk

    def phase_scores(p, diagonal):
        ys = []
        for s, hh, ci in jobs:
            q = q_ref[rows(s, ci), lanes(hh)]
            k = k_ref[pl.ds(key_start(s, p), key_width(ci, diagonal)), lanes(hh)]
            ys.append(lax.dot_general(q, k, (((1,), (1,)), ((), ())), preferred_element_type=F32) * scale2)
        return ys

    def phase_tails(ys, diagonal):
        pasts, tails = [], []
        for (s, hh, ci), y in zip(jobs, ys):
            kw = key_width(ci, diagonal)
            sp = jnp.maximum(y, 0.0) + jnp.log2(1.0 + jnp.exp2(_neg_abs(y)))
            if diagonal:
                pasts.append(iota2((r_blk, kw), 1) < iota2((r_blk, kw), 0) + ci * r_blk)
                sp = jnp.where(pasts[-1], sp, 0.0)
            hi, lo = _split_bf16(sp)
            ones = ones_full if kw == t_blk else suffix_ones(kw)
            tails.append(_dot(hi, ones) + _dot(lo, ones))
        return tails, pasts

    def phase_output(p, ys, tails, pasts, cs, accs, diagonal):
        new_cs, new_accs = [], []
        for n, (s, hh, ci) in enumerate(jobs):
            c = cs[n] if diagonal else cs[n] + out_of_keys(s, p)
            w = jnp.exp2(jnp.minimum(ys[n] - tails[n] - c, 0.0))
            if diagonal:
                w = jnp.where(pasts[n], w, 0.0)
            v = v_ref[pl.ds(key_start(s, p), key_width(ci, diagonal)), lanes(hh)]
            new_accs.append(accs[n] + _dot(w.astype(BF16), v))
            new_cs.append(c + tails[n][:, 0:1])
        return tuple(new_cs), tuple(new_accs)

    def live(cs):
        worst = functools.reduce(jnp.maximum, [z_bound2(*job) - cs[n] for n, job in enumerate(jobs)])
        return jnp.max(worst) >= cutoff2

    y_d = phase_scores(0, True)
    y_p = phase_scores(1, False)
    tails_d, pasts_d = phase_tails(y_d, True)
    tails_p, _ = phase_tails(y_p, False)
    cs = tuple(jnp.zeros((r_blk, 1), F32) for _ in jobs)
    accs = tuple(jnp.zeros((r_blk, dh), F32) for _ in jobs)
    cs, accs = phase_output(0, y_d, tails_d, pasts_d, cs, accs, True)
    cs, accs = phase_output(1, y_p, tails_p, None, cs, accs, False)

    def cond(carry):
        p, go, _, _ = carry
        return jnp.logical_and(key_tile(n_sub - 1, p) >= 0, go)

    def body(carry):
        p, _, cs, accs = carry
        ys = phase_scores(p, False)
        tails, _ = phase_tails(ys, False)
        cs, accs = phase_output(p, ys, tails, None, cs, accs, False)
        return p + 1, live(cs), cs, accs

    _, _, _, accs = lax.while_loop(cond, body, (jnp.int32(2), live(cs), cs, accs))
    for n, (s, hh, ci) in enumerate(jobs):
        o_ref[rows(s, ci), lanes(hh)] = accs[n].astype(o_ref.dtype)


def _sb_attention(qkv, *, bsz, seq, q_blk, t_blk, r_blk):
    n, three_inner = qkv.shape
    inner = three_inner // 3
    dh = inner // SB_HEADS
    hp = SB_HEADS_PER_STEP
    groups = SB_HEADS // hp
    nq = seq // q_blk
    return pl.pallas_call(
        functools.partial(_sb_attn_kernel, t_blk=t_blk, r_blk=r_blk, dh=dh, scale=dh ** -0.5),
        grid=(bsz, groups, nq),
        in_specs=[
            pl.BlockSpec((q_blk, hp * dh), lambda b, h, i: (b * nq + i, h)),
            pl.BlockSpec((seq, hp * dh), lambda b, h, i: (b, groups + h)),
            pl.BlockSpec((seq, hp * dh), lambda b, h, i: (b, 2 * groups + h)),
        ],
        out_specs=pl.BlockSpec((q_blk, hp * dh), lambda b, h, i: (b * nq + i, h)),
        out_shape=jax.ShapeDtypeStruct((n, inner), BF16),
        scratch_shapes=[pltpu.SMEM((hp,), F32)],
        compiler_params=_params(("arbitrary", "arbitrary", "arbitrary")),
        name="sb_attn",
    )(qkv, qkv, qkv)


def _proj_res_ln_kernel(o_ref, w_ref, x_ref, g_ref, b_ref, out_ref):
    y = DEEPNORM_ALPHA * x_ref[...] + _dot(o_ref[...], w_ref[...])
    out_ref[...] = _layer_norm(y, g_ref[...], b_ref[...])


def _proj_res_ln(o, w, x, g, b, *, tm):
    n, d = x.shape
    inner = o.shape[1]
    return pl.pallas_call(
        _proj_res_ln_kernel,
        grid=(n // tm,),
        in_specs=[
            pl.BlockSpec((tm, inner), lambda i: (i, 0)),
            pl.BlockSpec((inner, d), lambda i: (0, 0)),
            pl.BlockSpec((tm, d), lambda i: (i, 0)),
            pl.BlockSpec((1, d), lambda i: (0, 0)),
            pl.BlockSpec((1, d), lambda i: (0, 0)),
        ],
        out_specs=pl.BlockSpec((tm, d), lambda i: (i, 0)),
        out_shape=jax.ShapeDtypeStruct((n, d), F32),
        compiler_params=_params(("arbitrary",)),
        name="proj_res_ln",
    )(o, w, x, g, b)


def _pool_mix_kernel(u_ref, halo_ref, gate_ref, x_ref, wgrp_ref, scale_ref, wout_ref, g_ref, b_ref,
                     out_ref, t_ref, *, tiles_per_seq):
    tm, d = x_ref.shape
    cg = d // len(POOL_WINDOWS)
    tile_in_seq = pl.program_id(0) % tiles_per_seq

    pos = tile_in_seq * tm + lax.broadcasted_iota(jnp.int32, (tm, 1), 0)
    halo_in_seq = tile_in_seq * tm - POOL_HALO + lax.broadcasted_iota(jnp.int32, (POOL_HALO, 1), 0) >= 0

    for gi, window in enumerate(POOL_WINDOWS):
        cols = slice(gi * cg, (gi + 1) * cg)
        u = u_ref[:, cols]
        halo = jnp.where(halo_in_seq, halo_ref[:, cols], 0.0)
        s = jnp.concatenate([halo, u], axis=0)
        span = 1
        while span < window:
            s = s + pltpu.roll(s, span, axis=0)
            span *= 2
        count = jnp.minimum(pos + 1, window).astype(F32)
        pooled = s[POOL_HALO:] / count - u
        mixed = _dot(pooled.astype(BF16), wgrp_ref[gi])
        t_ref[:, cols] = (scale_ref[:, cols] * mixed * _silu(gate_ref[:, cols])).astype(BF16)

    y = DEEPNORM_ALPHA * x_ref[...] + _dot(t_ref[...], wout_ref[...])
    out_ref[...] = _layer_norm(y, g_ref[...], b_ref[...])


def _pool_mix(ug, x, w_grp, scale, w_out, g, b, *, seq, tm):
    n, d = x.shape
    groups, cg, _ = w_grp.shape
    halo_per_tile = tm // POOL_HALO
    return pl.pallas_call(
        functools.partial(_pool_mix_kernel, tiles_per_seq=seq // tm),
        grid=(n // tm,),
        in_specs=[
            pl.BlockSpec((tm, d), lambda i: (i, 0)),
            pl.BlockSpec((POOL_HALO, d), lambda i: (jnp.maximum(i * halo_per_tile - 1, 0), 0)),
            pl.BlockSpec((tm, d), lambda i: (i, 1)),
            pl.BlockSpec((tm, d), lambda i: (i, 0)),
            pl.BlockSpec((groups, cg, cg), lambda i: (0, 0, 0)),
            pl.BlockSpec((1, d), lambda i: (0, 0)),
            pl.BlockSpec((d, d), lambda i: (0, 0)),
            pl.BlockSpec((1, d), lambda i: (0, 0)),
            pl.BlockSpec((1, d), lambda i: (0, 0)),
        ],
        out_specs=pl.BlockSpec((tm, d), lambda i: (i, 0)),
        out_shape=jax.ShapeDtypeStruct((n, d), F32),
        scratch_shapes=[pltpu.VMEM((tm, d), BF16)],
        compiler_params=_params(("arbitrary",)),
        name="pool_mix",
    )(ug, ug, ug, x, w_grp, scale, w_out, g, b)


def _tile(n, preferred):
    t = preferred
    while n % t:
        t //= 2
    return t


def kernel(x, l0_ffn1_w_in, l0_ffn1_w_out, l0_ln1_g, l0_ln1_b, l0_sb_w_qkv, l0_sb_w_o, l0_ln2_g, l0_ln2_b, l0_ffn2_w_in, l0_ffn2_w_out, l0_ln3_g, l0_ln3_b, l1_ffn1_w_in, l1_ffn1_w_out, l1_ln1_g, l1_ln1_b, l1_pool_w_in, l1_pool_w_grp, l1_pool_scale, l1_pool_w_out, l1_ln2_g, l1_ln2_b, l1_ffn2_w_in, l1_ffn2_w_out, l1_ln3_g, l1_ln3_b):
    bsz, seq, d = x.shape
    n = bsz * seq
    bf = lambda w: w.astype(BF16)
    vec = lambda p: p.reshape(1, -1).astype(F32)

    ffn = functools.partial(_ffn_ln, tm=_tile(n, 512), tf=_tile(l0_ffn1_w_out.shape[0], 1024))
    h = x.reshape(n, d)

    h = ffn(h, bf(l0_ffn1_w_in), bf(l0_ffn1_w_out), vec(l0_ln1_g), vec(l0_ln1_b))
    qkv = _proj(h, bf(l0_sb_w_qkv), BF16, tm=_tile(n, 1024), tn=_tile(l0_sb_w_qkv.shape[1], 1024))
    o = _sb_attention(qkv, bsz=bsz, seq=seq, q_blk=_tile(seq, 1024), t_blk=_tile(seq, 256), r_blk=128)
    h = _proj_res_ln(o, bf(l0_sb_w_o), h, vec(l0_ln2_g), vec(l0_ln2_b), tm=_tile(n, 512))
    h = ffn(h, bf(l0_ffn2_w_in), bf(l0_ffn2_w_out), vec(l0_ln3_g), vec(l0_ln3_b))

    h = ffn(h, bf(l1_ffn1_w_in), bf(l1_ffn1_w_out), vec(l1_ln1_g), vec(l1_ln1_b))
    ug = _proj(h, bf(l1_pool_w_in), F32, tm=_tile(n, 1024), tn=_tile(l1_pool_w_in.shape[1], 1024))
    h = _pool_mix(ug, h, bf(l1_pool_w_grp), vec(l1_pool_scale), bf(l1_pool_w_out), vec(l1_ln2_g), vec(l1_ln2_b),
                  seq=seq, tm=_tile(seq, 256))
    h = ffn(h, bf(l1_ffn2_w_in), bf(l1_ffn2_w_out), vec(l1_ln3_g), vec(l1_ln3_b))
    return h.reshape(bsz, seq, d)
```

```python
import functools
import math

import jax
import jax.numpy as jnp
from jax import lax
from jax.experimental import pallas as pl
from jax.experimental.pallas import tpu as pltpu

F32 = jnp.float32
BF16 = jnp.bfloat16

DEPTH = 2
DEEPNORM_ALPHA = (2 * DEPTH) ** 0.25
FFN_RESIDUAL = 0.5
LN_EPS = 1e-5
SB_HEADS = 8
SB_HEADS_PER_STEP = 2
POOL_WINDOWS = (2, 4, 8, 16)
POOL_HALO = max(POOL_WINDOWS)

LOG2E = math.log2(math.e)
F32_EXP_ZERO_BELOW = -104.0
CUTOFF_MARGIN = 1.0
NO_KEYS = 1e30
NORM_SLACK = 1.01

VMEM_LIMIT = 56 * 1024 * 1024


def _params(semantics):
    return pltpu.CompilerParams(dimension_semantics=semantics, vmem_limit_bytes=VMEM_LIMIT)


def _layer_norm(y, g, b, eps=LN_EPS):
    mu = jnp.mean(y, axis=-1, keepdims=True)
    d = y - mu
    var = jnp.mean(d * d, axis=-1, keepdims=True)
    return d * lax.rsqrt(var + eps) * g + b


def _silu(a):
    return a * jax.nn.sigmoid(a)


def _dot(a, b):
    return jnp.dot(a, b, preferred_element_type=F32)


def _bits(x):
    return lax.bitcast_convert_type(x, jnp.uint32)


def _split_bf16(x):
    hi = lax.bitcast_convert_type(_bits(x) & jnp.uint32(0xFFFF0000), F32)
    return hi.astype(BF16), (x - hi).astype(BF16)


def _neg_abs(x):
    return lax.bitcast_convert_type(_bits(x) | jnp.uint32(0x80000000), F32)


def _ffn_ln_kernel(*refs, n_tiles, n_cast):
    x_ref, wa_ref, wg_ref, wo_ref, g_ref, b_ref = refs[:6]
    cast_src = refs[6:6 + n_cast]
    o_ref = refs[6 + n_cast]
    cast_dst = refs[7 + n_cast:7 + 2 * n_cast]
    xb_ref, acc_ref = refs[7 + 2 * n_cast:]
    g = pl.program_id(0)
    j = pl.program_id(1)
    mm_slot = (g + 1) % 2
    io_slot = g % 2
    rows = x_ref.shape[0]
    slice_j = pl.ds(pl.multiple_of(j * rows, rows), rows)

    def normalise_slice():
        a = acc_ref[io_slot, slice_j, :]
        o_ref[...] = _layer_norm(a, g_ref[...], b_ref[...], eps=LN_EPS / FFN_RESIDUAL ** 2)

    def stage_accumulator_slice():
        acc_ref[io_slot, slice_j, :] = (DEEPNORM_ALPHA / FFN_RESIDUAL) * x_ref[...]

    def stage_bf16_slice():
        xb_ref[io_slot, slice_j, :] = x_ref[...].astype(BF16)

    @pl.when(g == 0)
    def _():
        acc_ref[1, slice_j, :] = jnp.zeros((rows, acc_ref.shape[2]), F32)
        stage_accumulator_slice()
        stage_bf16_slice()

    @pl.when(jnp.logical_and(g >= 1, g <= n_tiles))
    def _():
        normalise_slice()
        stage_accumulator_slice()
        xb = xb_ref[mm_slot]
        a = _dot(xb, wa_ref[...])
        gate = _dot(xb, wg_ref[...])
        stage_bf16_slice()
        h = (_silu(a) * gate).astype(BF16)
        acc_ref[mm_slot] += _dot(h, wo_ref[...])
        for src, dst in zip(cast_src, cast_dst):
            dst[...] = src[...].astype(BF16)

    @pl.when(g == n_tiles + 1)
    def _():
        normalise_slice()


BF16_SUBLANES = 16


def _ffn_ln(x, w_in, w_out, g, b, cast=(), *, tm, tf):
    n, d = x.shape
    d_ff = w_out.shape[0]
    nj = d_ff // tf
    n_tiles = n // tm
    steps = n_tiles * nj
    rows = tm // nj
    assert tm % nj == 0 and rows % BF16_SUBLANES == 0

    def cast_spec(c):
        r, cols = c.shape
        blk = -(-r // (steps * BF16_SUBLANES)) * BF16_SUBLANES
        assert r % blk == 0
        nb = r // blk

        def block(g, j):
            step = jnp.clip((g - 1) * nj + j, 0, steps - 1)
            return step * nb // steps, 0

        return pl.BlockSpec((blk, cols), block)

    cast_specs = [cast_spec(c) for c in cast]
    outs = pl.pallas_call(
        functools.partial(_ffn_ln_kernel, n_tiles=n_tiles, n_cast=len(cast)),
        grid=(n_tiles + 2, nj),
        in_specs=[
            pl.BlockSpec((rows, d), lambda g, j: (jnp.minimum(g, n_tiles - 1) * nj + j, 0)),
            pl.BlockSpec((d, tf), lambda g, j: (0, j)),
            pl.BlockSpec((d, tf), lambda g, j: (0, nj + j)),
            pl.BlockSpec((tf, d), lambda g, j: (j, 0)),
            pl.BlockSpec((1, d), lambda g, j: (0, 0)),
            pl.BlockSpec((1, d), lambda g, j: (0, 0)),
        ] + cast_specs,
        out_specs=[pl.BlockSpec((rows, d), lambda g, j: (jnp.maximum((g - 2) * nj + j, 0), 0))] + cast_specs,
        out_shape=[jax.ShapeDtypeStruct((n, d), F32)] + [jax.ShapeDtypeStruct(c.shape, BF16) for c in cast],
        scratch_shapes=[pltpu.VMEM((2, tm, d), BF16), pltpu.VMEM((2, tm, d), F32)],
        compiler_params=_params(("arbitrary", "arbitrary")),
        name="ffn_ln",
    )(x, w_in, w_in, w_out, g, b, *cast)
    return tuple(outs)


def _proj_kernel(x_ref, w_ref, o_ref, xb_ref):
    @pl.when(pl.program_id(1) == 0)
    def _():
        xb_ref[...] = x_ref[...].astype(BF16)

    o_ref[...] = _dot(xb_ref[...], w_ref[...]).astype(o_ref.dtype)


def _proj(x, w, out_dtype, *, tm, tn):
    n, d = x.shape
    n_out = w.shape[1]
    return pl.pallas_call(
        _proj_kernel,
        grid=(n // tm, n_out // tn),
        in_specs=[
            pl.BlockSpec((tm, d), lambda i, j: (i, 0)),
            pl.BlockSpec((d, tn), lambda i, j: (0, j)),
        ],
        out_specs=pl.BlockSpec((tm, tn), lambda i, j: (i, j)),
        out_shape=jax.ShapeDtypeStruct((n, n_out), out_dtype),
        scratch_shapes=[pltpu.VMEM((tm, d), BF16)],
        compiler_params=_params(("arbitrary", "arbitrary")),
        name="proj",
    )(x, w)


def _sb_attn_kernel(q_ref, k_ref, v_ref, o_ref, kmax2_ref, *, t_blk, r_blk, dh, scale):
    qi = pl.program_id(2)
    n_kblk = k_ref.shape[0] // t_blk
    n_sub = q_ref.shape[0] // t_blk
    heads = range(q_ref.shape[1] // dh)
    chunks = range(t_blk // r_blk)
    jobs = [(s, hh, ci) for s in range(n_sub) for hh in heads for ci in chunks]
    scale2 = scale * LOG2E
    cutoff2 = (F32_EXP_ZERO_BELOW - CUTOFF_MARGIN) * LOG2E

    def lanes(hh):
        return slice(hh * dh, (hh + 1) * dh)

    def rows(s, ci):
        r0 = s * t_blk + ci * r_blk
        return slice(r0, r0 + r_blk)

    def key_tile(s, p):
        return qi * n_sub + s - p

    def key_start(s, p):
        return pl.multiple_of(jnp.maximum(key_tile(s, p), 0) * t_blk, t_blk)

    def out_of_keys(s, p):
        return jnp.where(key_tile(s, p) < 0, jnp.float32(NO_KEYS), jnp.float32(0.0))

    @pl.when(qi == 0)
    def _():
        ones = jnp.ones((dh, dh), BF16)

        def body(i, ms):
            k_rows = pl.ds(pl.multiple_of(i * t_blk, t_blk), t_blk)
            ks = [k_ref[k_rows, lanes(hh)] for hh in heads]
            return tuple(jnp.maximum(m, _dot(k * k, ones)) for m, k in zip(ms, ks))

        ms = lax.fori_loop(0, n_kblk, body, tuple(jnp.zeros((t_blk, dh), F32) for _ in heads),
                           unroll=math.gcd(n_kblk, 8))
        for hh in heads:
            kmax2_ref[hh] = jnp.max(ms[hh]) * NORM_SLACK

    def iota2(shape, axis):
        return lax.broadcasted_iota(jnp.int32, shape, axis)

    def suffix_ones(kw):
        return (iota2((kw, kw), 0) >= iota2((kw, kw), 1)).astype(BF16)

    ones_full = suffix_ones(t_blk)

    def z_bound2(s, hh, ci):
        qf = q_ref[rows(s, ci), lanes(hh)].astype(F32)
        return jnp.sqrt(jnp.sum(qf * qf, axis=-1, keepdims=True) * kmax2_ref[hh]) * scale2

    def key_width(ci, diagonal):
        return min(t_blk, -(-((ci + 1) * r_blk) // dh) * dh) if diagonal else t_blk

    def phase_scores(p, diagonal):
        ys = []
        for s, hh, ci in jobs:
            q = q_ref[rows(s, ci), lanes(hh)]
            k = k_ref[pl.ds(key_start(s, p), key_width(ci, diagonal)), lanes(hh)]
            ys.append(lax.dot_general(q, k, (((1,), (1,)), ((), ())), preferred_element_type=F32) * scale2)
        return ys

    def phase_tails(ys, diagonal):
        pasts, tails = [], []
        for (s, hh, ci), y in zip(jobs, ys):
            kw = key_width(ci, diagonal)
            sp = jnp.maximum(y, 0.0) + jnp.log2(1.0 + jnp.exp2(_neg_abs(y)))
            if diagonal:
                pasts.append(iota2((r_blk, kw), 1) < iota2((r_blk, kw), 0) + ci * r_blk)
                sp = jnp.where(pasts[-1], sp, 0.0)
            hi, lo = _split_bf16(sp)
            ones = ones_full if kw == t_blk else suffix_ones(kw)
            tails.append(_dot(hi, ones) + _dot(lo, ones))
        return tails, pasts

    def phase_output(p, ys, tails, pasts, cs, accs, diagonal):
        new_cs, new_accs = [], []
        for n, (s, hh, ci) in enumerate(jobs):
            c = cs[n] if diagonal else cs[n] + out_of_keys(s, p)
            w = jnp.exp2(jnp.minimum(ys[n] - tails[n] - c, 0.0))
            if diagonal:
                w = jnp.where(pasts[n], w, 0.0)
            v = v_ref[pl.ds(key_start(s, p), key_width(ci, diagonal)), lanes(hh)]
            new_accs.append(accs[n] + _dot(w.astype(BF16), v))
            new_cs.append(c + tails[n][:, 0:1])
        return tuple(new_cs), tuple(new_accs)

    def live(cs):
        worst = functools.reduce(jnp.maximum, [z_bound2(*job) - cs[n] for n, job in enumerate(jobs)])
        return jnp.max(worst) >= cutoff2

    y_d = phase_scores(0, True)
    y_p = phase_scores(1, False)
    tails_d, pasts_d = phase_tails(y_d, True)
    tails_p, _ = phase_tails(y_p, False)
    cs = tuple(jnp.zeros((r_blk, 1), F32) for _ in jobs)
    accs = tuple(jnp.zeros((r_blk, dh), F32) for _ in jobs)
    cs, accs = phase_output(0, y_d, tails_d, pasts_d, cs, accs, True)
    cs, accs = phase_output(1, y_p, tails_p, None, cs, accs, False)

    def cond(carry):
        p, go, _, _ = carry
        return jnp.logical_and(key_tile(n_sub - 1, p) >= 0, go)

    def body(carry):
        p, _, cs, accs = carry
        ys = phase_scores(p, False)
        tails, _ = phase_tails(ys, False)
        cs, accs = phase_output(p, ys, tails, None, cs, accs, False)
        return p + 1, live(cs), cs, accs

    _, _, _, accs = lax.while_loop(cond, body, (jnp.int32(2), live(cs), cs, accs))
    for n, (s, hh, ci) in enumerate(jobs):
        o_ref[rows(s, ci), lanes(hh)] = accs[n].astype(o_ref.dtype)


def _sb_attention(qkv, *, bsz, seq, q_blk, t_blk, r_blk):
    n, three_inner = qkv.shape
    inner = three_inner // 3
    dh = inner // SB_HEADS
    hp = SB_HEADS_PER_STEP
    groups = SB_HEADS // hp
    nq = seq // q_blk
    return pl.pallas_call(
        functools.partial(_sb_attn_kernel, t_blk=t_blk, r_blk=r_blk, dh=dh, scale=dh ** -0.5),
        grid=(bsz, groups, nq),
        in_specs=[
            pl.BlockSpec((q_blk, hp * dh), lambda b, h, i: (b * nq + i, h)),
            pl.BlockSpec((seq, hp * dh), lambda b, h, i: (b, groups + h)),
            pl.BlockSpec((seq, hp * dh), lambda b, h, i: (b, 2 * groups + h)),
        ],
        out_specs=pl.BlockSpec((q_blk, hp * dh), lambda b, h, i: (b * nq + i, h)),
        out_shape=jax.ShapeDtypeStruct((n, inner), BF16),
        scratch_shapes=[pltpu.SMEM((hp,), F32)],
        compiler_params=_params(("arbitrary", "arbitrary", "arbitrary")),
        name="sb_attn",
    )(qkv, qkv, qkv)


def _proj_res_ln_kernel(o_ref, w_ref, x_ref, g_ref, b_ref, out_ref):
    y = DEEPNORM_ALPHA * x_ref[...] + _dot(o_ref[...], w_ref[...])
    out_ref[...] = _layer_norm(y, g_ref[...], b_ref[...])


def _proj_res_ln(o, w, x, g, b, *, tm):
    n, d = x.shape
    inner = o.shape[1]
    return pl.pallas_call(
        _proj_res_ln_kernel,
        grid=(n // tm,),
        in_specs=[
            pl.BlockSpec((tm, inner), lambda i: (i, 0)),
            pl.BlockSpec((inner, d), lambda i: (0, 0)),
            pl.BlockSpec((tm, d), lambda i: (i, 0)),
            pl.BlockSpec((1, d), lambda i: (0, 0)),
            pl.BlockSpec((1, d), lambda i: (0, 0)),
        ],
        out_specs=pl.BlockSpec((tm, d), lambda i: (i, 0)),
        out_shape=jax.ShapeDtypeStruct((n, d), F32),
        compiler_params=_params(("arbitrary",)),
        name="proj_res_ln",
    )(o, w, x, g, b)


def _pool_mix_kernel(u_ref, halo_ref, gate_ref, x_ref, wgrp_ref, scale_ref, wout_ref, g_ref, b_ref,
                     out_ref, t_ref, *, tiles_per_seq):
    tm, d = x_ref.shape
    cg = d // len(POOL_WINDOWS)
    tile_in_seq = pl.program_id(0) % tiles_per_seq

    pos = tile_in_seq * tm + lax.broadcasted_iota(jnp.int32, (tm, 1), 0)
    halo_in_seq = tile_in_seq * tm - POOL_HALO + lax.broadcasted_iota(jnp.int32, (POOL_HALO, 1), 0) >= 0

    for gi, window in enumerate(POOL_WINDOWS):
        cols = slice(gi * cg, (gi + 1) * cg)
        u = u_ref[:, cols]
        halo = jnp.where(halo_in_seq, halo_ref[:, cols], 0.0)
        s = jnp.concatenate([halo, u], axis=0)
        span = 1
        while span < window:
            s = s + pltpu.roll(s, span, axis=0)
            span *= 2
        count = jnp.minimum(pos + 1, window).astype(F32)
        pooled = s[POOL_HALO:] / count - u
        mixed = _dot(pooled.astype(BF16), wgrp_ref[gi])
        t_ref[:, cols] = (scale_ref[:, cols] * mixed * _silu(gate_ref[:, cols])).astype(BF16)

    y = DEEPNORM_ALPHA * x_ref[...] + _dot(t_ref[...], wout_ref[...])
    out_ref[...] = _layer_norm(y, g_ref[...], b_ref[...])


def _pool_mix(ug, x, w_grp, scale, w_out, g, b, *, seq, tm):
    n, d = x.shape
    groups, cg, _ = w_grp.shape
    halo_per_tile = tm // POOL_HALO
    return pl.pallas_call(
        functools.partial(_pool_mix_kernel, tiles_per_seq=seq // tm),
        grid=(n // tm,),
        in_specs=[
            pl.BlockSpec((tm, d), lambda i: (i, 0)),
            pl.BlockSpec((POOL_HALO, d), lambda i: (jnp.maximum(i * halo_per_tile - 1, 0), 0)),
            pl.BlockSpec((tm, d), lambda i: (i, 1)),
            pl.BlockSpec((tm, d), lambda i: (i, 0)),
            pl.BlockSpec((groups, cg, cg), lambda i: (0, 0, 0)),
            pl.BlockSpec((1, d), lambda i: (0, 0)),
            pl.BlockSpec((d, d), lambda i: (0, 0)),
            pl.BlockSpec((1, d), lambda i: (0, 0)),
            pl.BlockSpec((1, d), lambda i: (0, 0)),
        ],
        out_specs=pl.BlockSpec((tm, d), lambda i: (i, 0)),
        out_shape=jax.ShapeDtypeStruct((n, d), F32),
        scratch_shapes=[pltpu.VMEM((tm, d), BF16)],
        compiler_params=_params(("arbitrary",)),
        name="pool_mix",
    )(ug, ug, ug, x, w_grp, scale, w_out, g, b)


def _tile(n, preferred):
    t = preferred
    while n % t:
        t //= 2
    return t


def kernel(x, l0_ffn1_w_in, l0_ffn1_w_out, l0_ln1_g, l0_ln1_b, l0_sb_w_qkv, l0_sb_w_o, l0_ln2_g, l0_ln2_b, l0_ffn2_w_in, l0_ffn2_w_out, l0_ln3_g, l0_ln3_b, l1_ffn1_w_in, l1_ffn1_w_out, l1_ln1_g, l1_ln1_b, l1_pool_w_in, l1_pool_w_grp, l1_pool_scale, l1_pool_w_out, l1_ln2_g, l1_ln2_b, l1_ffn2_w_in, l1_ffn2_w_out, l1_ln3_g, l1_ln3_b):
    bsz, seq, d = x.shape
    n = bsz * seq
    bf = lambda w: w.astype(BF16)
    vec = lambda p: p.reshape(1, -1).astype(F32)

    ffn = functools.partial(_ffn_ln, tm=_tile(n, 512), tf=_tile(l0_ffn1_w_out.shape[0], 1024))
    h = x.reshape(n, d)

    h, w_in, w_out = ffn(h, bf(l0_ffn1_w_in), bf(l0_ffn1_w_out), vec(l0_ln1_g), vec(l0_ln1_b),
                         cast=(l0_ffn2_w_in, l0_ffn2_w_out))
    qkv = _proj(h, bf(l0_sb_w_qkv), BF16, tm=_tile(n, 1024), tn=_tile(l0_sb_w_qkv.shape[1], 1024))
    o = _sb_attention(qkv, bsz=bsz, seq=seq, q_blk=_tile(seq, 1024), t_blk=_tile(seq, 256), r_blk=128)
    h = _proj_res_ln(o, bf(l0_sb_w_o), h, vec(l0_ln2_g), vec(l0_ln2_b), tm=_tile(n, 512))
    h, w_in, w_out = ffn(h, w_in, w_out, vec(l0_ln3_g), vec(l0_ln3_b), cast=(l1_ffn1_w_in, l1_ffn1_w_out))

    h, w_in, w_out = ffn(h, w_in, w_out, vec(l1_ln1_g), vec(l1_ln1_b), cast=(l1_ffn2_w_in, l1_ffn2_w_out))
    ug = _proj(h, bf(l1_pool_w_in), F32, tm=_tile(n, 1024), tn=_tile(l1_pool_w_in.shape[1], 1024))
    h = _pool_mix(ug, h, bf(l1_pool_w_grp), vec(l1_pool_scale), bf(l1_pool_w_out), vec(l1_ln2_g), vec(l1_ln2_b),
                  seq=seq, tm=_tile(seq, 256))
    h, = ffn(h, w_in, w_out, vec(l1_ln3_g), vec(l1_ln3_b))
    return h.reshape(bsz, seq, d)
```

```python
import functools
import math

import jax
import jax.numpy as jnp
from jax import lax
from jax.experimental import pallas as pl
from jax.experimental.pallas import tpu as pltpu

F32 = jnp.float32
BF16 = jnp.bfloat16

DEPTH = 2
DEEPNORM_ALPHA = (2 * DEPTH) ** 0.25
FFN_RESIDUAL = 0.5
LN_EPS = 1e-5
SB_HEADS = 8
SB_HEADS_PER_STEP = 2
POOL_WINDOWS = (2, 4, 8, 16)
POOL_HALO = max(POOL_WINDOWS)

LOG2E = math.log2(math.e)
F32_EXP_ZERO_BELOW = -104.0
CUTOFF_MARGIN = 1.0
NO_KEYS = 1e30
NORM_SLACK = 1.01

VMEM_LIMIT = 56 * 1024 * 1024


def _params(semantics):
    return pltpu.CompilerParams(dimension_semantics=semantics, vmem_limit_bytes=VMEM_LIMIT)


def _layer_norm(y, g, b, eps=LN_EPS):
    mu = jnp.mean(y, axis=-1, keepdims=True)
    d = y - mu
    var = jnp.mean(d * d, axis=-1, keepdims=True)
    return d * lax.rsqrt(var + eps) * g + b


def _silu(a):
    return a * jax.nn.sigmoid(a)


def _dot(a, b):
    return jnp.dot(a, b, preferred_element_type=F32)


def _bits(x):
    return lax.bitcast_convert_type(x, jnp.uint32)


def _split_bf16(x):
    hi = lax.bitcast_convert_type(_bits(x) & jnp.uint32(0xFFFF0000), F32)
    return hi.astype(BF16), (x - hi).astype(BF16)


def _neg_abs(x):
    return lax.bitcast_convert_type(_bits(x) | jnp.uint32(0x80000000), F32)


def _interleave_value_gate(w_in, tf):
    d, two_ff = w_in.shape
    nj = two_ff // (2 * tf)
    return w_in.reshape(d, 2, nj, tf).transpose(0, 2, 1, 3).reshape(d, two_ff)


def _ffn_ln_kernel(*refs, n_tiles, cast_interleave, tf):
    n_cast = len(cast_interleave)
    x_ref, w_ref, wo_ref, g_ref, b_ref = refs[:5]
    cast_src = refs[5:5 + n_cast]
    o_ref = refs[5 + n_cast]
    cast_dst = refs[6 + n_cast:6 + 2 * n_cast]
    xb_ref, acc_ref = refs[6 + 2 * n_cast:]
    i = pl.program_id(0)
    j = pl.program_id(1)
    cur = i % 2
    rows = x_ref.shape[0] // pl.num_programs(1)

    def normalise_slice_of_previous_tile():
        a = acc_ref[1 - cur, pl.ds(pl.multiple_of(j * rows, rows), rows), :]
        o_ref[...] = _layer_norm(a, g_ref[...], b_ref[...], eps=LN_EPS / FFN_RESIDUAL ** 2)

    def ride_along_cast(src, dst, interleave):
        if not interleave:
            dst[...] = src[...].astype(BF16)
            return
        d_ff = src.shape[1] // 2
        for c in range(d_ff // tf):
            dst[:, 2 * c * tf:(2 * c + 1) * tf] = src[:, c * tf:(c + 1) * tf].astype(BF16)
            dst[:, (2 * c + 1) * tf:(2 * c + 2) * tf] = src[:, d_ff + c * tf:d_ff + (c + 1) * tf].astype(BF16)

    @pl.when(i < n_tiles)
    def _():
        @pl.when(j == 0)
        def _():
            x = x_ref[...]
            xb_ref[...] = x.astype(BF16)
            acc_ref[cur] = (DEEPNORM_ALPHA / FFN_RESIDUAL) * x

        @pl.when(jnp.logical_and(i == 0, j == 0))
        def _():
            acc_ref[1] = jnp.zeros(acc_ref.shape[1:], F32)

        xb = xb_ref[...]
        a = _dot(xb, w_ref[:, :tf])
        normalise_slice_of_previous_tile()
        gate = _dot(xb, w_ref[:, tf:])
        h = (_silu(a) * gate).astype(BF16)
        acc_ref[cur] += _dot(h, wo_ref[...])
        for src, dst, interleave in zip(cast_src, cast_dst, cast_interleave):
            ride_along_cast(src, dst, interleave)

    @pl.when(i == n_tiles)
    def _():
        normalise_slice_of_previous_tile()


BF16_SUBLANES = 16


def _ffn_ln(x, w_in, w_out, g, b, next_w_in=None, next_w_out=None, *, tm, tf):
    n, d = x.shape
    d_ff = w_out.shape[0]
    nj = d_ff // tf
    n_tiles = n // tm
    steps = n_tiles * nj
    assert tm % nj == 0 and (tm // nj) % 8 == 0
    cast = () if next_w_in is None else (next_w_in, next_w_out)
    cast_interleave = () if next_w_in is None else (True, False)

    def cast_spec(c):
        r, cols = c.shape
        blk = -(-r // (steps * BF16_SUBLANES)) * BF16_SUBLANES
        assert r % blk == 0
        nb = r // blk
        return pl.BlockSpec((blk, cols), lambda i, j: (jnp.minimum((i * nj + j) * nb // steps, nb - 1), 0))

    cast_specs = [cast_spec(c) for c in cast]
    outs = pl.pallas_call(
        functools.partial(_ffn_ln_kernel, n_tiles=n_tiles, cast_interleave=cast_interleave, tf=tf),
        grid=(n_tiles + 1, nj),
        in_specs=[
            pl.BlockSpec((tm, d), lambda i, j: (jnp.minimum(i, n_tiles - 1), 0)),
            pl.BlockSpec((d, 2 * tf), lambda i, j: (0, j)),
            pl.BlockSpec((tf, d), lambda i, j: (j, 0)),
            pl.BlockSpec((1, d), lambda i, j: (0, 0)),
            pl.BlockSpec((1, d), lambda i, j: (0, 0)),
        ] + cast_specs,
        out_specs=[pl.BlockSpec((tm // nj, d), lambda i, j: (jnp.maximum((i - 1) * nj + j, 0), 0))] + cast_specs,
        out_shape=[jax.ShapeDtypeStruct((n, d), F32)] + [jax.ShapeDtypeStruct(c.shape, BF16) for c in cast],
        scratch_shapes=[pltpu.VMEM((tm, d), BF16), pltpu.VMEM((2, tm, d), F32)],
        compiler_params=_params(("arbitrary", "arbitrary")),
        name="ffn_ln",
    )(x, w_in, w_out, g, b, *cast)
    return tuple(outs)


def _proj_kernel(x_ref, w_ref, o_ref, xb_ref):
    @pl.when(pl.program_id(1) == 0)
    def _():
        xb_ref[...] = x_ref[...].astype(BF16)

    o_ref[...] = _dot(xb_ref[...], w_ref[...]).astype(o_ref.dtype)


def _proj(x, w, out_dtype, *, tm, tn):
    n, d = x.shape
    n_out = w.shape[1]
    return pl.pallas_call(
        _proj_kernel,
        grid=(n // tm, n_out // tn),
        in_specs=[
            pl.BlockSpec((tm, d), lambda i, j: (i, 0)),
            pl.BlockSpec((d, tn), lambda i, j: (0, j)),
        ],
        out_specs=pl.BlockSpec((tm, tn), lambda i, j: (i, j)),
        out_shape=jax.ShapeDtypeStruct((n, n_out), out_dtype),
        scratch_shapes=[pltpu.VMEM((tm, d), BF16)],
        compiler_params=_params(("arbitrary", "arbitrary")),
        name="proj",
    )(x, w)


def _sb_attn_kernel(q_ref, k_ref, v_ref, o_ref, kmax2_ref, *, t_blk, r_blk, dh, scale):
    qi = pl.program_id(2)
    n_kblk = k_ref.shape[0] // t_blk
    n_sub = q_ref.shape[0] // t_blk
    heads = range(q_ref.shape[1] // dh)
    chunks = range(t_blk // r_blk)
    jobs = [(s, hh, ci) for s in range(n_sub) for hh in heads for ci in chunks]
    scale2 = scale * LOG2E
    cutoff2 = (F32_EXP_ZERO_BELOW - CUTOFF_MARGIN) * LOG2E

    def lanes(hh):
        return slice(hh * dh, (hh + 1) * dh)

    def rows(s, ci):
        r0 = s * t_blk + ci * r_blk
        return slice(r0, r0 + r_blk)

    def key_tile(s, p):
        return qi * n_sub + s - p

    def key_start(s, p):
        return pl.multiple_of(jnp.maximum(key_tile(s, p), 0) * t_blk, t_blk)

    def out_of_keys(s, p):
        return jnp.where(key_tile(s, p) < 0, jnp.float32(NO_KEYS), jnp.float32(0.0))

    @pl.when(qi == 0)
    def _():
        ones = jnp.ones((dh, dh), BF16)

        def body(i, ms):
            k_rows = pl.ds(pl.multiple_of(i * t_blk, t_blk), t_blk)
            ks = [k_ref[k_rows, lanes(hh)] for hh in heads]
            return tuple(jnp.maximum(m, _dot(k * k, ones)) for m, k in zip(ms, ks))

        ms = lax.fori_loop(0, n_kblk, body, tuple(jnp.zeros((t_blk, dh), F32) for _ in heads),
                           unroll=math.gcd(n_kblk, 8))
        for hh in heads:
            kmax2_ref[hh] = jnp.max(ms[hh]) * NORM_SLACK

    def iota2(shape, axis):
        return lax.broadcasted_iota(jnp.int32, shape, axis)

    def suffix_ones(kw):
        return (iota2((kw, kw), 0) >= iota2((kw, kw), 1)).astype(BF16)

    ones_full = suffix_ones(t_blk)

    def z_bound2(s, hh, ci):
        qf = q_ref[rows(s, ci), lanes(hh)].astype(F32)
        return jnp.sqrt(jnp.sum(qf * qf, axis=-1, keepdims=True) * kmax2_ref[hh]) * scale2

    def key_width(ci, diagonal):
        return min(t_blk, -(-((ci + 1) * r_blk) // dh) * dh) if diagonal else t_blk

    def phase_scores(p, diagonal):
        ys = []
        for s, hh, ci in jobs:
            q = q_ref[rows(s, ci), lanes(hh)]
            k = k_ref[pl.ds(key_start(s, p), key_width(ci, diagonal)), lanes(hh)]
            ys.append(lax.dot_general(q, k, (((1,), (1,)), ((), ())), preferred_element_type=F32) * scale2)
        return ys

    def phase_tails(ys, diagonal):
        pasts, tails = [], []
        for (s, hh, ci), y in zip(jobs, ys):
            kw = key_width(ci, diagonal)
            sp = jnp.maximum(y, 0.0) + jnp.log2(1.0 + jnp.exp2(_neg_abs(y)))
            if diagonal:
                pasts.append(iota2((r_blk, kw), 1) < iota2((r_blk, kw), 0) + ci * r_blk)
                sp = jnp.where(pasts[-1], sp, 0.0)
            hi, lo = _split_bf16(sp)
            ones = ones_full if kw == t_blk else suffix_ones(kw)
            tails.append(_dot(hi, ones) + _dot(lo, ones))
        return tails, pasts

    def phase_output(p, ys, tails, pasts, cs, accs, diagonal):
        new_cs, new_accs = [], []
        for n, (s, hh, ci) in enumerate(jobs):
            c = cs[n] if diagonal else cs[n] + out_of_keys(s, p)
            w = jnp.exp2(jnp.minimum(ys[n] - tails[n] - c, 0.0))
            if diagonal:
                w = jnp.where(pasts[n], w, 0.0)
            v = v_ref[pl.ds(key_start(s, p), key_width(ci, diagonal)), lanes(hh)]
            new_accs.append(accs[n] + _dot(w.astype(BF16), v))
            new_cs.append(c + tails[n][:, 0:1])
        return tuple(new_cs), tuple(new_accs)

    def live(cs):
        worst = functools.reduce(jnp.maximum, [z_bound2(*job) - cs[n] for n, job in enumerate(jobs)])
        return jnp.max(worst) >= cutoff2

    y_d = phase_scores(0, True)
    y_p = phase_scores(1, False)
    tails_d, pasts_d = phase_tails(y_d, True)
    tails_p, _ = phase_tails(y_p, False)
    cs = tuple(jnp.zeros((r_blk, 1), F32) for _ in jobs)
    accs = tuple(jnp.zeros((r_blk, dh), F32) for _ in jobs)
    cs, accs = phase_output(0, y_d, tails_d, pasts_d, cs, accs, True)
    cs, accs = phase_output(1, y_p, tails_p, None, cs, accs, False)

    def cond(carry):
        p, go, _, _ = carry
        return jnp.logical_and(key_tile(n_sub - 1, p) >= 0, go)

    def body(carry):
        p, _, cs, accs = carry
        ys = phase_scores(p, False)
        tails, _ = phase_tails(ys, False)
        cs, accs = phase_output(p, ys, tails, None, cs, accs, False)
        return p + 1, live(cs), cs, accs

    _, _, _, accs = lax.while_loop(cond, body, (jnp.int32(2), live(cs), cs, accs))
    for n, (s, hh, ci) in enumerate(jobs):
        o_ref[rows(s, ci), lanes(hh)] = accs[n].astype(o_ref.dtype)


def _sb_attention(qkv, *, bsz, seq, q_blk, t_blk, r_blk):
    n, three_inner = qkv.shape
    inner = three_inner // 3
    dh = inner // SB_HEADS
    hp = SB_HEADS_PER_STEP
    groups = SB_HEADS // hp
    nq = seq // q_blk
    return pl.pallas_call(
        functools.partial(_sb_attn_kernel, t_blk=t_blk, r_blk=r_blk, dh=dh, scale=dh ** -0.5),
        grid=(bsz, groups, nq),
        in_specs=[
            pl.BlockSpec((q_blk, hp * dh), lambda b, h, i: (b * nq + i, h)),
            pl.BlockSpec((seq, hp * dh), lambda b, h, i: (b, groups + h)),
            pl.BlockSpec((seq, hp * dh), lambda b, h, i: (b, 2 * groups + h)),
        ],
        out_specs=pl.BlockSpec((q_blk, hp * dh), lambda b, h, i: (b * nq + i, h)),
        out_shape=jax.ShapeDtypeStruct((n, inner), BF16),
        scratch_shapes=[pltpu.SMEM((hp,), F32)],
        compiler_params=_params(("arbitrary", "arbitrary", "arbitrary")),
        name="sb_attn",
    )(qkv, qkv, qkv)


def _proj_res_ln_kernel(o_ref, w_ref, x_ref, g_ref, b_ref, out_ref):
    y = DEEPNORM_ALPHA * x_ref[...] + _dot(o_ref[...], w_ref[...])
    out_ref[...] = _layer_norm(y, g_ref[...], b_ref[...])


def _proj_res_ln(o, w, x, g, b, *, tm):
    n, d = x.shape
    inner = o.shape[1]
    return pl.pallas_call(
        _proj_res_ln_kernel,
        grid=(n // tm,),
        in_specs=[
            pl.BlockSpec((tm, inner), lambda i: (i, 0)),
            pl.BlockSpec((inner, d), lambda i: (0, 0)),
            pl.BlockSpec((tm, d), lambda i: (i, 0)),
            pl.BlockSpec((1, d), lambda i: (0, 0)),
            pl.BlockSpec((1, d), lambda i: (0, 0)),
        ],
        out_specs=pl.BlockSpec((tm, d), lambda i: (i, 0)),
        out_shape=jax.ShapeDtypeStruct((n, d), F32),
        compiler_params=_params(("arbitrary",)),
        name="proj_res_ln",
    )(o, w, x, g, b)


def _pool_mix_kernel(u_ref, halo_ref, gate_ref, x_ref, wgrp_ref, scale_ref, wout_ref, g_ref, b_ref,
                     out_ref, t_ref, *, tiles_per_seq):
    tm, d = x_ref.shape
    cg = d // len(POOL_WINDOWS)
    tile_in_seq = pl.program_id(0) % tiles_per_seq

    pos = tile_in_seq * tm + lax.broadcasted_iota(jnp.int32, (tm, 1), 0)
    halo_in_seq = tile_in_seq * tm - POOL_HALO + lax.broadcasted_iota(jnp.int32, (POOL_HALO, 1), 0) >= 0

    for gi, window in enumerate(POOL_WINDOWS):
        cols = slice(gi * cg, (gi + 1) * cg)
        u = u_ref[:, cols]
        halo = jnp.where(halo_in_seq, halo_ref[:, cols], 0.0)
        s = jnp.concatenate([halo, u], axis=0)
        span = 1
        while span < window:
            s = s + pltpu.roll(s, span, axis=0)
            span *= 2
        count = jnp.minimum(pos + 1, window).astype(F32)
        pooled = s[POOL_HALO:] / count - u
        mixed = _dot(pooled.astype(BF16), wgrp_ref[gi])
        t_ref[:, cols] = (scale_ref[:, cols] * mixed * _silu(gate_ref[:, cols])).astype(BF16)

    y = DEEPNORM_ALPHA * x_ref[...] + _dot(t_ref[...], wout_ref[...])
    out_ref[...] = _layer_norm(y, g_ref[...], b_ref[...])


def _pool_mix(ug, x, w_grp, scale, w_out, g, b, *, seq, tm):
    n, d = x.shape
    groups, cg, _ = w_grp.shape
    halo_per_tile = tm // POOL_HALO
    return pl.pallas_call(
        functools.partial(_pool_mix_kernel, tiles_per_seq=seq // tm),
        grid=(n // tm,),
        in_specs=[
            pl.BlockSpec((tm, d), lambda i: (i, 0)),
            pl.BlockSpec((POOL_HALO, d), lambda i: (jnp.maximum(i * halo_per_tile - 1, 0), 0)),
            pl.BlockSpec((tm, d), lambda i: (i, 1)),
            pl.BlockSpec((tm, d), lambda i: (i, 0)),
            pl.BlockSpec((groups, cg, cg), lambda i: (0, 0, 0)),
            pl.BlockSpec((1, d), lambda i: (0, 0)),
            pl.BlockSpec((d, d), lambda i: (0, 0)),
            pl.BlockSpec((1, d), lambda i: (0, 0)),
            pl.BlockSpec((1, d), lambda i: (0, 0)),
        ],
        out_specs=pl.BlockSpec((tm, d), lambda i: (i, 0)),
        out_shape=jax.ShapeDtypeStruct((n, d), F32),
        scratch_shapes=[pltpu.VMEM((tm, d), BF16)],
        compiler_params=_params(("arbitrary",)),
        name="pool_mix",
    )(ug, ug, ug, x, w_grp, scale, w_out, g, b)


def _tile(n, preferred):
    t = preferred
    while n % t:
        t //= 2
    return t


def kernel(x, l0_ffn1_w_in, l0_ffn1_w_out, l0_ln1_g, l0_ln1_b, l0_sb_w_qkv, l0_sb_w_o, l0_ln2_g, l0_ln2_b, l0_ffn2_w_in, l0_ffn2_w_out, l0_ln3_g, l0_ln3_b, l1_ffn1_w_in, l1_ffn1_w_out, l1_ln1_g, l1_ln1_b, l1_pool_w_in, l1_pool_w_grp, l1_pool_scale, l1_pool_w_out, l1_ln2_g, l1_ln2_b, l1_ffn2_w_in, l1_ffn2_w_out, l1_ln3_g, l1_ln3_b):
    bsz, seq, d = x.shape
    n = bsz * seq
    bf = lambda w: w.astype(BF16)
    vec = lambda p: p.reshape(1, -1).astype(F32)

    tf = _tile(l0_ffn1_w_out.shape[0], 1024)
    ffn = functools.partial(_ffn_ln, tm=_tile(n, 512), tf=tf)
    h = x.reshape(n, d)

    h, w_in, w_out = ffn(h, _interleave_value_gate(bf(l0_ffn1_w_in), tf), bf(l0_ffn1_w_out),
                         vec(l0_ln1_g), vec(l0_ln1_b), l0_ffn2_w_in, l0_ffn2_w_out)
    qkv = _proj(h, bf(l0_sb_w_qkv), BF16, tm=_tile(n, 1024), tn=_tile(l0_sb_w_qkv.shape[1], 1024))
    o = _sb_attention(qkv, bsz=bsz, seq=seq, q_blk=_tile(seq, 1024), t_blk=_tile(seq, 256), r_blk=128)
    h = _proj_res_ln(o, bf(l0_sb_w_o), h, vec(l0_ln2_g), vec(l0_ln2_b), tm=_tile(n, 512))
    h, w_in, w_out = ffn(h, w_in, w_out, vec(l0_ln3_g), vec(l0_ln3_b), l1_ffn1_w_in, l1_ffn1_w_out)

    h, w_in, w_out = ffn(h, w_in, w_out, vec(l1_ln1_g), vec(l1_ln1_b), l1_ffn2_w_in, l1_ffn2_w_out)
    ug = _proj(h, bf(l1_pool_w_in), F32, tm=_tile(n, 1024), tn=_tile(l1_pool_w_in.shape[1], 1024))
    h = _pool_mix(ug, h, bf(l1_pool_w_grp), vec(l1_pool_scale), bf(l1_pool_w_out), vec(l1_ln2_g), vec(l1_ln2_b),
                  seq=seq, tm=_tile(seq, 256))
    h, = ffn(h, w_in, w_out, vec(l1_ln3_g), vec(l1_ln3_b))
    return h.reshape(bsz, seq, d)
```

```python
import functools
import math

import jax
import jax.numpy as jnp
from jax import lax
from jax.experimental import pallas as pl
from jax.experimental.pallas import tpu as pltpu

F32 = jnp.float32
BF16 = jnp.bfloat16

DEPTH = 2
DEEPNORM_ALPHA = (2 * DEPTH) ** 0.25
FFN_RESIDUAL = 0.5
LN_EPS = 1e-5
SB_HEADS = 8
SB_HEADS_PER_STEP = 2
POOL_WINDOWS = (2, 4, 8, 16)
POOL_HALO = max(POOL_WINDOWS)

LOG2E = math.log2(math.e)
F32_EXP_ZERO_BELOW = -104.0
CUTOFF_MARGIN = 1.0
NO_KEYS = 1e30
NORM_SLACK = 1.01

VMEM_LIMIT = 56 * 1024 * 1024


def _params(semantics):
    return pltpu.CompilerParams(dimension_semantics=semantics, vmem_limit_bytes=VMEM_LIMIT)


def _layer_norm(y, g, b, eps=LN_EPS):
    mu = jnp.mean(y, axis=-1, keepdims=True)
    d = y - mu
    var = jnp.mean(d * d, axis=-1, keepdims=True)
    return d * lax.rsqrt(var + eps) * g + b


def _silu(a):
    return a * jax.nn.sigmoid(a)


def _dot(a, b):
    return jnp.dot(a, b, preferred_element_type=F32)


def _bits(x):
    return lax.bitcast_convert_type(x, jnp.uint32)


def _split_bf16(x):
    hi = lax.bitcast_convert_type(_bits(x) & jnp.uint32(0xFFFF0000), F32)
    return hi.astype(BF16), (x - hi).astype(BF16)


def _neg_abs(x):
    return lax.bitcast_convert_type(_bits(x) | jnp.uint32(0x80000000), F32)


def _cast_value_gate_interleaved(src, dst, tf):
    d_ff = src.shape[1] // 2
    for c in range(d_ff // tf):
        dst[:, 2 * c * tf:(2 * c + 1) * tf] = src[:, c * tf:(c + 1) * tf].astype(BF16)
        dst[:, (2 * c + 1) * tf:(2 * c + 2) * tf] = src[:, d_ff + c * tf:d_ff + (c + 1) * tf].astype(BF16)


def _interleave_value_gate(w_in, tf, *, row_block=256):
    r, cols = w_in.shape
    blk = _tile(r, row_block)
    spec = pl.BlockSpec((blk, cols), lambda i: (i, 0))
    return pl.pallas_call(
        functools.partial(_cast_value_gate_interleaved, tf=tf),
        grid=(r // blk,),
        in_specs=[spec],
        out_specs=spec,
        out_shape=jax.ShapeDtypeStruct(w_in.shape, BF16),
        compiler_params=_params(("arbitrary",)),
        name="cast_w_in",
    )(w_in)


def _ffn_ln_kernel(*refs, n_tiles, cast_interleave, tf):
    n_cast = len(cast_interleave)
    x_ref, w_ref, wo_ref, g_ref, b_ref = refs[:5]
    cast_src = refs[5:5 + n_cast]
    o_ref = refs[5 + n_cast]
    cast_dst = refs[6 + n_cast:6 + 2 * n_cast]
    xb_ref, acc_ref = refs[6 + 2 * n_cast:]
    i = pl.program_id(0)
    j = pl.program_id(1)
    cur = i % 2
    rows = x_ref.shape[0] // pl.num_programs(1)

    def normalise_slice_of_previous_tile():
        a = acc_ref[1 - cur, pl.ds(pl.multiple_of(j * rows, rows), rows), :]
        o_ref[...] = _layer_norm(a, g_ref[...], b_ref[...], eps=LN_EPS / FFN_RESIDUAL ** 2)

    def ride_along_cast(src, dst, interleave):
        if interleave:
            _cast_value_gate_interleaved(src, dst, tf)
        else:
            dst[...] = src[...].astype(BF16)

    @pl.when(i < n_tiles)
    def _():
        @pl.when(j == 0)
        def _():
            x = x_ref[...]
            xb_ref[...] = x.astype(BF16)
            acc_ref[cur] = (DEEPNORM_ALPHA / FFN_RESIDUAL) * x

        @pl.when(jnp.logical_and(i == 0, j == 0))
        def _():
            acc_ref[1] = jnp.zeros(acc_ref.shape[1:], F32)

        xb = xb_ref[...]
        a = _dot(xb, w_ref[:, :tf])
        normalise_slice_of_previous_tile()
        gate = _dot(xb, w_ref[:, tf:])
        h = (_silu(a) * gate).astype(BF16)
        acc_ref[cur] += _dot(h, wo_ref[...])
        for src, dst, interleave in zip(cast_src, cast_dst, cast_interleave):
            ride_along_cast(src, dst, interleave)

    @pl.when(i == n_tiles)
    def _():
        normalise_slice_of_previous_tile()


BF16_SUBLANES = 16


def _ffn_ln(x, w_in, w_out, g, b, next_w_in=None, next_w_out=None, *, tm, tf):
    n, d = x.shape
    d_ff = w_out.shape[0]
    nj = d_ff // tf
    n_tiles = n // tm
    steps = n_tiles * nj
    assert tm % nj == 0 and (tm // nj) % 8 == 0
    cast = () if next_w_in is None else (next_w_in, next_w_out)
    cast_interleave = () if next_w_in is None else (True, False)

    def cast_spec(c):
        r, cols = c.shape
        blk = -(-r // (steps * BF16_SUBLANES)) * BF16_SUBLANES
        assert r % blk == 0
        nb = r // blk
        return pl.BlockSpec((blk, cols), lambda i, j: (jnp.minimum((i * nj + j) * nb // steps, nb - 1), 0))

    cast_specs = [cast_spec(c) for c in cast]
    outs = pl.pallas_call(
        functools.partial(_ffn_ln_kernel, n_tiles=n_tiles, cast_interleave=cast_interleave, tf=tf),
        grid=(n_tiles + 1, nj),
        in_specs=[
            pl.BlockSpec((tm, d), lambda i, j: (jnp.minimum(i, n_tiles - 1), 0)),
            pl.BlockSpec((d, 2 * tf), lambda i, j: (0, j)),
            pl.BlockSpec((tf, d), lambda i, j: (j, 0)),
            pl.BlockSpec((1, d), lambda i, j: (0, 0)),
            pl.BlockSpec((1, d), lambda i, j: (0, 0)),
        ] + cast_specs,
        out_specs=[pl.BlockSpec((tm // nj, d), lambda i, j: (jnp.maximum((i - 1) * nj + j, 0), 0))] + cast_specs,
        out_shape=[jax.ShapeDtypeStruct((n, d), F32)] + [jax.ShapeDtypeStruct(c.shape, BF16) for c in cast],
        scratch_shapes=[pltpu.VMEM((tm, d), BF16), pltpu.VMEM((2, tm, d), F32)],
        compiler_params=_params(("arbitrary", "arbitrary")),
        name="ffn_ln",
    )(x, w_in, w_out, g, b, *cast)
    return tuple(outs)


def _proj_kernel(x_ref, w_ref, o_ref, xb_ref):
    @pl.when(pl.program_id(1) == 0)
    def _():
        xb_ref[...] = x_ref[...].astype(BF16)

    o_ref[...] = _dot(xb_ref[...], w_ref[...]).astype(o_ref.dtype)


def _proj(x, w, out_dtype, *, tm, tn):
    n, d = x.shape
    n_out = w.shape[1]
    return pl.pallas_call(
        _proj_kernel,
        grid=(n // tm, n_out // tn),
        in_specs=[
            pl.BlockSpec((tm, d), lambda i, j: (i, 0)),
            pl.BlockSpec((d, tn), lambda i, j: (0, j)),
        ],
        out_specs=pl.BlockSpec((tm, tn), lambda i, j: (i, j)),
        out_shape=jax.ShapeDtypeStruct((n, n_out), out_dtype),
        scratch_shapes=[pltpu.VMEM((tm, d), BF16)],
        compiler_params=_params(("arbitrary", "arbitrary")),
        name="proj",
    )(x, w)


def _sb_attn_kernel(q_ref, k_ref, v_ref, o_ref, kmax2_ref, *, t_blk, r_blk, dh, scale):
    qi = pl.program_id(2)
    n_kblk = k_ref.shape[0] // t_blk
    n_sub = q_ref.shape[0] // t_blk
    heads = range(q_ref.shape[1] // dh)
    chunks = range(t_blk // r_blk)
    jobs = [(s, hh, ci) for s in range(n_sub) for hh in heads for ci in chunks]
    scale2 = scale * LOG2E
    cutoff2 = (F32_EXP_ZERO_BELOW - CUTOFF_MARGIN) * LOG2E

    def lanes(hh):
        return slice(hh * dh, (hh + 1) * dh)

    def rows(s, ci):
        r0 = s * t_blk + ci * r_blk
        return slice(r0, r0 + r_blk)

    def key_tile(s, p):
        return qi * n_sub + s - p

    def key_start(s, p):
        return pl.multiple_of(jnp.maximum(key_tile(s, p), 0) * t_blk, t_blk)

    def out_of_keys(s, p):
        return jnp.where(key_tile(s, p) < 0, jnp.float32(NO_KEYS), jnp.float32(0.0))

    @pl.when(qi == 0)
    def _():
        ones = jnp.ones((dh, dh), BF16)

        def body(i, ms):
            k_rows = pl.ds(pl.multiple_of(i * t_blk, t_blk), t_blk)
            ks = [k_ref[k_rows, lanes(hh)] for hh in heads]
            return tuple(jnp.maximum(m, _dot(k * k, ones)) for m, k in zip(ms, ks))

        ms = lax.fori_loop(0, n_kblk, body, tuple(jnp.zeros((t_blk, dh), F32) for _ in heads),
                           unroll=math.gcd(n_kblk, 8))
        for hh in heads:
            kmax2_ref[hh] = jnp.max(ms[hh]) * NORM_SLACK

    def iota2(shape, axis):
        return lax.broadcasted_iota(jnp.int32, shape, axis)

    def suffix_ones(kw):
        return (iota2((kw, kw), 0) >= iota2((kw, kw), 1)).astype(BF16)

    ones_full = suffix_ones(t_blk)

    def z_bound2(s, hh, ci):
        qf = q_ref[rows(s, ci), lanes(hh)].astype(F32)
        return jnp.sqrt(jnp.sum(qf * qf, axis=-1, keepdims=True) * kmax2_ref[hh]) * scale2

    def key_width(ci, diagonal):
        return min(t_blk, -(-((ci + 1) * r_blk) // dh) * dh) if diagonal else t_blk

    def phase_scores(p, diagonal):
        ys = []
        for s, hh, ci in jobs:
            q = q_ref[rows(s, ci), lanes(hh)]
            k = k_ref[pl.ds(key_start(s, p), key_width(ci, diagonal)), lanes(hh)]
            ys.append(lax.dot_general(q, k, (((1,), (1,)), ((), ())), preferred_element_type=F32) * scale2)
        return ys

    def phase_tails(ys, diagonal):
        pasts, tails = [], []
        for (s, hh, ci), y in zip(jobs, ys):
            kw = key_width(ci, diagonal)
            sp = jnp.maximum(y, 0.0) + jnp.log2(1.0 + jnp.exp2(_neg_abs(y)))
            if diagonal:
                pasts.append(iota2((r_blk, kw), 1) < iota2((r_blk, kw), 0) + ci * r_blk)
                sp = jnp.where(pasts[-1], sp, 0.0)
            hi, lo = _split_bf16(sp)
            ones = ones_full if kw == t_blk else suffix_ones(kw)
            tails.append(_dot(hi, ones) + _dot(lo, ones))
        return tails, pasts

    def phase_output(p, ys, tails, pasts, cs, accs, diagonal):
        new_cs, new_accs = [], []
        for n, (s, hh, ci) in enumerate(jobs):
            c = cs[n] if diagonal else cs[n] + out_of_keys(s, p)
            w = jnp.exp2(jnp.minimum(ys[n] - tails[n] - c, 0.0))
            if diagonal:
                w = jnp.where(pasts[n], w, 0.0)
            v = v_ref[pl.ds(key_start(s, p), key_width(ci, diagonal)), lanes(hh)]
            new_accs.append(accs[n] + _dot(w.astype(BF16), v))
            new_cs.append(c + tails[n][:, 0:1])
        return tuple(new_cs), tuple(new_accs)

    def live(cs):
        worst = functools.reduce(jnp.maximum, [z_bound2(*job) - cs[n] for n, job in enumerate(jobs)])
        return jnp.max(worst) >= cutoff2

    y_d = phase_scores(0, True)
    y_p = phase_scores(1, False)
    tails_d, pasts_d = phase_tails(y_d, True)
    tails_p, _ = phase_tails(y_p, False)
    cs = tuple(jnp.zeros((r_blk, 1), F32) for _ in jobs)
    accs = tuple(jnp.zeros((r_blk, dh), F32) for _ in jobs)
    cs, accs = phase_output(0, y_d, tails_d, pasts_d, cs, accs, True)
    cs, accs = phase_output(1, y_p, tails_p, None, cs, accs, False)

    def cond(carry):
        p, go, _, _ = carry
        return jnp.logical_and(key_tile(n_sub - 1, p) >= 0, go)

    def body(carry):
        p, _, cs, accs = carry
        ys = phase_scores(p, False)
        tails, _ = phase_tails(ys, False)
        cs, accs = phase_output(p, ys, tails, None, cs, accs, False)
        return p + 1, live(cs), cs, accs

    _, _, _, accs = lax.while_loop(cond, body, (jnp.int32(2), live(cs), cs, accs))
    for n, (s, hh, ci) in enumerate(jobs):
        o_ref[rows(s, ci), lanes(hh)] = accs[n].astype(o_ref.dtype)


def _sb_attention(qkv, *, bsz, seq, q_blk, t_blk, r_blk):
    n, three_inner = qkv.shape
    inner = three_inner // 3
    dh = inner // SB_HEADS
    hp = SB_HEADS_PER_STEP
    groups = SB_HEADS // hp
    nq = seq // q_blk
    return pl.pallas_call(
        functools.partial(_sb_attn_kernel, t_blk=t_blk, r_blk=r_blk, dh=dh, scale=dh ** -0.5),
        grid=(bsz, groups, nq),
        in_specs=[
            pl.BlockSpec((q_blk, hp * dh), lambda b, h, i: (b * nq + i, h)),
            pl.BlockSpec((seq, hp * dh), lambda b, h, i: (b, groups + h)),
            pl.BlockSpec((seq, hp * dh), lambda b, h, i: (b, 2 * groups + h)),
        ],
        out_specs=pl.BlockSpec((q_blk, hp * dh), lambda b, h, i: (b * nq + i, h)),
        out_shape=jax.ShapeDtypeStruct((n, inner), BF16),
        scratch_shapes=[pltpu.SMEM((hp,), F32)],
        compiler_params=_params(("arbitrary", "arbitrary", "arbitrary")),
        name="sb_attn",
    )(qkv, qkv, qkv)


def _proj_res_ln_kernel(o_ref, w_ref, x_ref, g_ref, b_ref, out_ref):
    y = DEEPNORM_ALPHA * x_ref[...] + _dot(o_ref[...], w_ref[...])
    out_ref[...] = _layer_norm(y, g_ref[...], b_ref[...])


def _proj_res_ln(o, w, x, g, b, *, tm):
    n, d = x.shape
    inner = o.shape[1]
    return pl.pallas_call(
        _proj_res_ln_kernel,
        grid=(n // tm,),
        in_specs=[
            pl.BlockSpec((tm, inner), lambda i: (i, 0)),
            pl.BlockSpec((inner, d), lambda i: (0, 0)),
            pl.BlockSpec((tm, d), lambda i: (i, 0)),
            pl.BlockSpec((1, d), lambda i: (0, 0)),
            pl.BlockSpec((1, d), lambda i: (0, 0)),
        ],
        out_specs=pl.BlockSpec((tm, d), lambda i: (i, 0)),
        out_shape=jax.ShapeDtypeStruct((n, d), F32),
        compiler_params=_params(("arbitrary",)),
        name="proj_res_ln",
    )(o, w, x, g, b)


def _pool_mix_kernel(u_ref, halo_ref, gate_ref, x_ref, wgrp_ref, scale_ref, wout_ref, g_ref, b_ref,
                     out_ref, t_ref, *, tiles_per_seq):
    tm, d = x_ref.shape
    cg = d // len(POOL_WINDOWS)
    tile_in_seq = pl.program_id(0) % tiles_per_seq

    pos = tile_in_seq * tm + lax.broadcasted_iota(jnp.int32, (tm, 1), 0)
    halo_in_seq = tile_in_seq * tm - POOL_HALO + lax.broadcasted_iota(jnp.int32, (POOL_HALO, 1), 0) >= 0

    for gi, window in enumerate(POOL_WINDOWS):
        cols = slice(gi * cg, (gi + 1) * cg)
        u = u_ref[:, cols]
        halo = jnp.where(halo_in_seq, halo_ref[:, cols], 0.0)
        s = jnp.concatenate([halo, u], axis=0)
        span = 1
        while span < window:
            s = s + pltpu.roll(s, span, axis=0)
            span *= 2
        count = jnp.minimum(pos + 1, window).astype(F32)
        pooled = s[POOL_HALO:] / count - u
        mixed = _dot(pooled.astype(BF16), wgrp_ref[gi])
        t_ref[:, cols] = (scale_ref[:, cols] * mixed * _silu(gate_ref[:, cols])).astype(BF16)

    y = DEEPNORM_ALPHA * x_ref[...] + _dot(t_ref[...], wout_ref[...])
    out_ref[...] = _layer_norm(y, g_ref[...], b_ref[...])


def _pool_mix(ug, x, w_grp, scale, w_out, g, b, *, seq, tm):
    n, d = x.shape
    groups, cg, _ = w_grp.shape
    halo_per_tile = tm // POOL_HALO
    return pl.pallas_call(
        functools.partial(_pool_mix_kernel, tiles_per_seq=seq // tm),
        grid=(n // tm,),
        in_specs=[
            pl.BlockSpec((tm, d), lambda i: (i, 0)),
            pl.BlockSpec((POOL_HALO, d), lambda i: (jnp.maximum(i * halo_per_tile - 1, 0), 0)),
            pl.BlockSpec((tm, d), lambda i: (i, 1)),
            pl.BlockSpec((tm, d), lambda i: (i, 0)),
            pl.BlockSpec((groups, cg, cg), lambda i: (0, 0, 0)),
            pl.BlockSpec((1, d), lambda i: (0, 0)),
            pl.BlockSpec((d, d), lambda i: (0, 0)),
            pl.BlockSpec((1, d), lambda i: (0, 0)),
            pl.BlockSpec((1, d), lambda i: (0, 0)),
        ],
        out_specs=pl.BlockSpec((tm, d), lambda i: (i, 0)),
        out_shape=jax.ShapeDtypeStruct((n, d), F32),
        scratch_shapes=[pltpu.VMEM((tm, d), BF16)],
        compiler_params=_params(("arbitrary",)),
        name="pool_mix",
    )(ug, ug, ug, x, w_grp, scale, w_out, g, b)


def _tile(n, preferred):
    t = preferred
    while n % t:
        t //= 2
    return t


def kernel(x, l0_ffn1_w_in, l0_ffn1_w_out, l0_ln1_g, l0_ln1_b, l0_sb_w_qkv, l0_sb_w_o, l0_ln2_g, l0_ln2_b, l0_ffn2_w_in, l0_ffn2_w_out, l0_ln3_g, l0_ln3_b, l1_ffn1_w_in, l1_ffn1_w_out, l1_ln1_g, l1_ln1_b, l1_pool_w_in, l1_pool_w_grp, l1_pool_scale, l1_pool_w_out, l1_ln2_g, l1_ln2_b, l1_ffn2_w_in, l1_ffn2_w_out, l1_ln3_g, l1_ln3_b):
    bsz, seq, d = x.shape
    n = bsz * seq
    bf = lambda w: w.astype(BF16)
    vec = lambda p: p.reshape(1, -1).astype(F32)

    tf = _tile(l0_ffn1_w_out.shape[0], 1024)
    ffn = functools.partial(_ffn_ln, tm=_tile(n, 512), tf=tf)
    h = x.reshape(n, d)

    h, w_in, w_out = ffn(h, _interleave_value_gate(l0_ffn1_w_in, tf), bf(l0_ffn1_w_out),
                         vec(l0_ln1_g), vec(l0_ln1_b), l0_ffn2_w_in, l0_ffn2_w_out)
    qkv = _proj(h, bf(l0_sb_w_qkv), BF16, tm=_tile(n, 1024), tn=_tile(l0_sb_w_qkv.shape[1], 1024))
    o = _sb_attention(qkv, bsz=bsz, seq=seq, q_blk=_tile(seq, 1024), t_blk=_tile(seq, 256), r_blk=128)
    h = _proj_res_ln(o, bf(l0_sb_w_o), h, vec(l0_ln2_g), vec(l0_ln2_b), tm=_tile(n, 512))
    h, w_in, w_out = ffn(h, w_in, w_out, vec(l0_ln3_g), vec(l0_ln3_b), l1_ffn1_w_in, l1_ffn1_w_out)

    h, w_in, w_out = ffn(h, w_in, w_out, vec(l1_ln1_g), vec(l1_ln1_b), l1_ffn2_w_in, l1_ffn2_w_out)
    ug = _proj(h, bf(l1_pool_w_in), F32, tm=_tile(n, 1024), tn=_tile(l1_pool_w_in.shape[1], 1024))
    h = _pool_mix(ug, h, bf(l1_pool_w_grp), vec(l1_pool_scale), bf(l1_pool_w_out), vec(l1_ln2_g), vec(l1_ln2_b),
                  seq=seq, tm=_tile(seq, 256))
    h, = ffn(h, w_in, w_out, vec(l1_ln3_g), vec(l1_ln3_b))
    return h.reshape(bsz, seq, d)
```

```python
import functools
import math

import jax
import jax.numpy as jnp
from jax import lax
from jax.experimental import pallas as pl
from jax.experimental.pallas import tpu as pltpu

F32 = jnp.float32
BF16 = jnp.bfloat16

DEPTH = 2
DEEPNORM_ALPHA = (2 * DEPTH) ** 0.25
FFN_RESIDUAL = 0.5
LN_EPS = 1e-5
SB_HEADS = 8
SB_HEADS_PER_STEP = 2
POOL_WINDOWS = (2, 4, 8, 16)
POOL_HALO = max(POOL_WINDOWS)

LOG2E = math.log2(math.e)
F32_EXP_ZERO_BELOW = -104.0
CUTOFF_MARGIN = 1.0
NO_KEYS = 1e30
NORM_SLACK = 1.01

VMEM_LIMIT = 56 * 1024 * 1024


def _params(semantics):
    return pltpu.CompilerParams(dimension_semantics=semantics, vmem_limit_bytes=VMEM_LIMIT)


def _layer_norm(y, g, b, eps=LN_EPS):
    mu = jnp.mean(y, axis=-1, keepdims=True)
    d = y - mu
    var = jnp.mean(d * d, axis=-1, keepdims=True)
    return d * lax.rsqrt(var + eps) * g + b


def _silu(a):
    return a * jax.nn.sigmoid(a)


def _dot(a, b):
    return jnp.dot(a, b, preferred_element_type=F32)


def _bits(x):
    return lax.bitcast_convert_type(x, jnp.uint32)


def _split_bf16(x):
    hi = lax.bitcast_convert_type(_bits(x) & jnp.uint32(0xFFFF0000), F32)
    return hi.astype(BF16), (x - hi).astype(BF16)


def _neg_abs(x):
    return lax.bitcast_convert_type(_bits(x) | jnp.uint32(0x80000000), F32)


def _cast_value_gate_interleaved(src, dst, tf):
    d_ff = src.shape[1] // 2
    for c in range(d_ff // tf):
        dst[:, 2 * c * tf:(2 * c + 1) * tf] = src[:, c * tf:(c + 1) * tf].astype(BF16)
        dst[:, (2 * c + 1) * tf:(2 * c + 2) * tf] = src[:, d_ff + c * tf:d_ff + (c + 1) * tf].astype(BF16)


def _interleave_value_gate(w_in, tf, *, row_block=256):
    r, cols = w_in.shape
    blk = _tile(r, row_block)
    spec = pl.BlockSpec((blk, cols), lambda i: (i, 0))
    return pl.pallas_call(
        functools.partial(_cast_value_gate_interleaved, tf=tf),
        grid=(r // blk,),
        in_specs=[spec],
        out_specs=spec,
        out_shape=jax.ShapeDtypeStruct(w_in.shape, BF16),
        compiler_params=_params(("arbitrary",)),
        name="cast_w_in",
    )(w_in)


def _ffn_ln_kernel(*refs, n_tiles, cast_interleave, tf):
    n_cast = len(cast_interleave)
    x_ref, w_ref, wo_ref, g_ref, b_ref = refs[:5]
    cast_src = refs[5:5 + n_cast]
    o_ref = refs[5 + n_cast]
    cast_dst = refs[6 + n_cast:6 + 2 * n_cast]
    xb_ref, acc_ref = refs[6 + 2 * n_cast:]
    i = pl.program_id(0)
    j = pl.program_id(1)
    cur = i % 2
    rows = x_ref.shape[0] // pl.num_programs(1)

    def normalise_slice_of_previous_tile():
        a = acc_ref[1 - cur, pl.ds(pl.multiple_of(j * rows, rows), rows), :]
        o_ref[...] = _layer_norm(a, g_ref[...], b_ref[...], eps=LN_EPS / FFN_RESIDUAL ** 2)

    def ride_along_cast(src, dst, interleave):
        if interleave:
            _cast_value_gate_interleaved(src, dst, tf)
        else:
            dst[...] = src[...].astype(BF16)

    @pl.when(i < n_tiles)
    def _():
        @pl.when(j == 0)
        def _():
            x = x_ref[...]
            xb_ref[...] = x.astype(BF16)
            acc_ref[cur] = (DEEPNORM_ALPHA / FFN_RESIDUAL) * x

        @pl.when(jnp.logical_and(i == 0, j == 0))
        def _():
            acc_ref[1] = jnp.zeros(acc_ref.shape[1:], F32)

        xb = xb_ref[...]
        a = _dot(xb, w_ref[:, :tf])
        normalise_slice_of_previous_tile()
        gate = _dot(xb, w_ref[:, tf:])
        h = (_silu(a) * gate).astype(BF16)
        acc_ref[cur] += _dot(h, wo_ref[...])
        for src, dst, interleave in zip(cast_src, cast_dst, cast_interleave):
            ride_along_cast(src, dst, interleave)

    @pl.when(i == n_tiles)
    def _():
        normalise_slice_of_previous_tile()


BF16_SUBLANES = 16
CAST_STEPS_PER_BLOCK = 2


def _ffn_ln(x, w_in, w_out, g, b, next_w_in=None, next_w_out=None, *, tm, tf):
    n, d = x.shape
    d_ff = w_out.shape[0]
    nj = d_ff // tf
    n_tiles = n // tm
    steps = n_tiles * nj
    assert tm % nj == 0 and (tm // nj) % 8 == 0
    cast = () if next_w_in is None else (next_w_in, next_w_out)
    cast_interleave = () if next_w_in is None else (True, False)

    def cast_spec(c):
        r, cols = c.shape
        blk = -(-r * CAST_STEPS_PER_BLOCK // (steps * BF16_SUBLANES)) * BF16_SUBLANES
        assert r % blk == 0
        nb = r // blk
        return pl.BlockSpec((blk, cols), lambda i, j: (jnp.minimum((i * nj + j) * nb // steps, nb - 1), 0))

    cast_specs = [cast_spec(c) for c in cast]
    outs = pl.pallas_call(
        functools.partial(_ffn_ln_kernel, n_tiles=n_tiles, cast_interleave=cast_interleave, tf=tf),
        grid=(n_tiles + 1, nj),
        in_specs=[
            pl.BlockSpec((tm, d), lambda i, j: (jnp.minimum(i, n_tiles - 1), 0)),
            pl.BlockSpec((d, 2 * tf), lambda i, j: (0, j)),
            pl.BlockSpec((tf, d), lambda i, j: (j, 0)),
            pl.BlockSpec((1, d), lambda i, j: (0, 0)),
            pl.BlockSpec((1, d), lambda i, j: (0, 0)),
        ] + cast_specs,
        out_specs=[pl.BlockSpec((tm // nj, d), lambda i, j: (jnp.maximum((i - 1) * nj + j, 0), 0))] + cast_specs,
        out_shape=[jax.ShapeDtypeStruct((n, d), F32)] + [jax.ShapeDtypeStruct(c.shape, BF16) for c in cast],
        scratch_shapes=[pltpu.VMEM((tm, d), BF16), pltpu.VMEM((2, tm, d), F32)],
        compiler_params=_params(("arbitrary", "arbitrary")),
        name="ffn_ln",
    )(x, w_in, w_out, g, b, *cast)
    return tuple(outs)


def _proj_kernel(x_ref, w_ref, o_ref, xb_ref):
    @pl.when(pl.program_id(1) == 0)
    def _():
        xb_ref[...] = x_ref[...].astype(BF16)

    o_ref[...] = _dot(xb_ref[...], w_ref[...]).astype(o_ref.dtype)


def _proj(x, w, out_dtype, *, tm, tn):
    n, d = x.shape
    n_out = w.shape[1]
    return pl.pallas_call(
        _proj_kernel,
        grid=(n // tm, n_out // tn),
        in_specs=[
            pl.BlockSpec((tm, d), lambda i, j: (i, 0)),
            pl.BlockSpec((d, tn), lambda i, j: (0, j)),
        ],
        out_specs=pl.BlockSpec((tm, tn), lambda i, j: (i, j)),
        out_shape=jax.ShapeDtypeStruct((n, n_out), out_dtype),
        scratch_shapes=[pltpu.VMEM((tm, d), BF16)],
        compiler_params=_params(("arbitrary", "arbitrary")),
        name="proj",
    )(x, w)


def _sb_attn_kernel(q_ref, k_ref, v_ref, o_ref, kmax2_ref, *, t_blk, r_blk, dh, scale):
    qi = pl.program_id(2)
    n_kblk = k_ref.shape[0] // t_blk
    n_sub = q_ref.shape[0] // t_blk
    heads = range(q_ref.shape[1] // dh)
    chunks = range(t_blk // r_blk)
    jobs = [(s, hh, ci) for s in range(n_sub) for hh in heads for ci in chunks]
    scale2 = scale * LOG2E
    cutoff2 = (F32_EXP_ZERO_BELOW - CUTOFF_MARGIN) * LOG2E

    def lanes(hh):
        return slice(hh * dh, (hh + 1) * dh)

    def rows(s, ci):
        r0 = s * t_blk + ci * r_blk
        return slice(r0, r0 + r_blk)

    def key_tile(s, p):
        return qi * n_sub + s - p

    def key_start(s, p):
        return pl.multiple_of(jnp.maximum(key_tile(s, p), 0) * t_blk, t_blk)

    def out_of_keys(s, p):
        return jnp.where(key_tile(s, p) < 0, jnp.float32(NO_KEYS), jnp.float32(0.0))

    @pl.when(qi == 0)
    def _():
        ones = jnp.ones((dh, dh), BF16)

        def body(i, ms):
            k_rows = pl.ds(pl.multiple_of(i * t_blk, t_blk), t_blk)
            ks = [k_ref[k_rows, lanes(hh)] for hh in heads]
            return tuple(jnp.maximum(m, _dot(k * k, ones)) for m, k in zip(ms, ks))

        ms = lax.fori_loop(0, n_kblk, body, tuple(jnp.zeros((t_blk, dh), F32) for _ in heads),
                           unroll=math.gcd(n_kblk, 8))
        for hh in heads:
            kmax2_ref[hh] = jnp.max(ms[hh]) * NORM_SLACK

    def iota2(shape, axis):
        return lax.broadcasted_iota(jnp.int32, shape, axis)

    def suffix_ones(kw):
        return (iota2((kw, kw), 0) >= iota2((kw, kw), 1)).astype(BF16)

    ones_full = suffix_ones(t_blk)

    def z_bound2(s, hh, ci):
        qf = q_ref[rows(s, ci), lanes(hh)].astype(F32)
        return jnp.sqrt(jnp.sum(qf * qf, axis=-1, keepdims=True) * kmax2_ref[hh]) * scale2

    def key_width(ci, diagonal):
        return min(t_blk, -(-((ci + 1) * r_blk) // dh) * dh) if diagonal else t_blk

    def phase_scores(p, diagonal):
        ys = []
        for s, hh, ci in jobs:
            q = q_ref[rows(s, ci), lanes(hh)]
            k = k_ref[pl.ds(key_start(s, p), key_width(ci, diagonal)), lanes(hh)]
            ys.append(lax.dot_general(q, k, (((1,), (1,)), ((), ())), preferred_element_type=F32) * scale2)
        return ys

    def phase_tails(ys, diagonal):
        pasts, tails = [], []
        for (s, hh, ci), y in zip(jobs, ys):
            kw = key_width(ci, diagonal)
            sp = jnp.maximum(y, 0.0) + jnp.log2(1.0 + jnp.exp2(_neg_abs(y)))
            if diagonal:
                pasts.append(iota2((r_blk, kw), 1) < iota2((r_blk, kw), 0) + ci * r_blk)
                sp = jnp.where(pasts[-1], sp, 0.0)
            hi, lo = _split_bf16(sp)
            ones = ones_full if kw == t_blk else suffix_ones(kw)
            tails.append(_dot(hi, ones) + _dot(lo, ones))
        return tails, pasts

    def phase_output(p, ys, tails, pasts, cs, accs, diagonal):
        new_cs, new_accs = [], []
        for n, (s, hh, ci) in enumerate(jobs):
            c = cs[n] if diagonal else cs[n] + out_of_keys(s, p)
            w = jnp.exp2(jnp.minimum(ys[n] - tails[n] - c, 0.0))
            if diagonal:
                w = jnp.where(pasts[n], w, 0.0)
            v = v_ref[pl.ds(key_start(s, p), key_width(ci, diagonal)), lanes(hh)]
            new_accs.append(accs[n] + _dot(w.astype(BF16), v))
            new_cs.append(c + tails[n][:, 0:1])
        return tuple(new_cs), tuple(new_accs)

    def live(cs):
        worst = functools.reduce(jnp.maximum, [z_bound2(*job) - cs[n] for n, job in enumerate(jobs)])
        return jnp.max(worst) >= cutoff2

    y_d = phase_scores(0, True)
    y_p = phase_scores(1, False)
    tails_d, pasts_d = phase_tails(y_d, True)
    tails_p, _ = phase_tails(y_p, False)
    cs = tuple(jnp.zeros((r_blk, 1), F32) for _ in jobs)
    accs = tuple(jnp.zeros((r_blk, dh), F32) for _ in jobs)
    cs, accs = phase_output(0, y_d, tails_d, pasts_d, cs, accs, True)
    cs, accs = phase_output(1, y_p, tails_p, None, cs, accs, False)

    def cond(carry):
        p, go, _, _ = carry
        return jnp.logical_and(key_tile(n_sub - 1, p) >= 0, go)

    def body(carry):
        p, _, cs, accs = carry
        ys = phase_scores(p, False)
        tails, _ = phase_tails(ys, False)
        cs, accs = phase_output(p, ys, tails, None, cs, accs, False)
        return p + 1, live(cs), cs, accs

    _, _, _, accs = lax.while_loop(cond, body, (jnp.int32(2), live(cs), cs, accs))
    for n, (s, hh, ci) in enumerate(jobs):
        o_ref[rows(s, ci), lanes(hh)] = accs[n].astype(o_ref.dtype)


def _sb_attention(qkv, *, bsz, seq, q_blk, t_blk, r_blk):
    n, three_inner = qkv.shape
    inner = three_inner // 3
    dh = inner // SB_HEADS
    hp = SB_HEADS_PER_STEP
    groups = SB_HEADS // hp
    nq = seq // q_blk
    return pl.pallas_call(
        functools.partial(_sb_attn_kernel, t_blk=t_blk, r_blk=r_blk, dh=dh, scale=dh ** -0.5),
        grid=(bsz, groups, nq),
        in_specs=[
            pl.BlockSpec((q_blk, hp * dh), lambda b, h, i: (b * nq + i, h)),
            pl.BlockSpec((seq, hp * dh), lambda b, h, i: (b, groups + h)),
            pl.BlockSpec((seq, hp * dh), lambda b, h, i: (b, 2 * groups + h)),
        ],
        out_specs=pl.BlockSpec((q_blk, hp * dh), lambda b, h, i: (b * nq + i, h)),
        out_shape=jax.ShapeDtypeStruct((n, inner), BF16),
        scratch_shapes=[pltpu.SMEM((hp,), F32)],
        compiler_params=_params(("arbitrary", "arbitrary", "arbitrary")),
        name="sb_attn",
    )(qkv, qkv, qkv)


def _proj_res_ln_kernel(o_ref, w_ref, x_ref, g_ref, b_ref, out_ref):
    y = DEEPNORM_ALPHA * x_ref[...] + _dot(o_ref[...], w_ref[...])
    out_ref[...] = _layer_norm(y, g_ref[...], b_ref[...])


def _proj_res_ln(o, w, x, g, b, *, tm):
    n, d = x.shape
    inner = o.shape[1]
    return pl.pallas_call(
        _proj_res_ln_kernel,
        grid=(n // tm,),
        in_specs=[
            pl.BlockSpec((tm, inner), lambda i: (i, 0)),
            pl.BlockSpec((inner, d), lambda i: (0, 0)),
            pl.BlockSpec((tm, d), lambda i: (i, 0)),
            pl.BlockSpec((1, d), lambda i: (0, 0)),
            pl.BlockSpec((1, d), lambda i: (0, 0)),
        ],
        out_specs=pl.BlockSpec((tm, d), lambda i: (i, 0)),
        out_shape=jax.ShapeDtypeStruct((n, d), F32),
        compiler_params=_params(("arbitrary",)),
        name="proj_res_ln",
    )(o, w, x, g, b)


def _pool_mix_kernel(u_ref, halo_ref, gate_ref, x_ref, wgrp_ref, scale_ref, wout_ref, g_ref, b_ref,
                     out_ref, t_ref, *, tiles_per_seq):
    tm, d = x_ref.shape
    cg = d // len(POOL_WINDOWS)
    tile_in_seq = pl.program_id(0) % tiles_per_seq

    pos = tile_in_seq * tm + lax.broadcasted_iota(jnp.int32, (tm, 1), 0)
    halo_in_seq = tile_in_seq * tm - POOL_HALO + lax.broadcasted_iota(jnp.int32, (POOL_HALO, 1), 0) >= 0

    for gi, window in enumerate(POOL_WINDOWS):
        cols = slice(gi * cg, (gi + 1) * cg)
        u = u_ref[:, cols]
        halo = jnp.where(halo_in_seq, halo_ref[:, cols], 0.0)
        s = jnp.concatenate([halo, u], axis=0)
        span = 1
        while span < window:
            s = s + pltpu.roll(s, span, axis=0)
            span *= 2
        count = jnp.minimum(pos + 1, window).astype(F32)
        pooled = s[POOL_HALO:] / count - u
        mixed = _dot(pooled.astype(BF16), wgrp_ref[gi])
        t_ref[:, cols] = (scale_ref[:, cols] * mixed * _silu(gate_ref[:, cols])).astype(BF16)

    y = DEEPNORM_ALPHA * x_ref[...] + _dot(t_ref[...], wout_ref[...])
    out_ref[...] = _layer_norm(y, g_ref[...], b_ref[...])


def _pool_mix(ug, x, w_grp, scale, w_out, g, b, *, seq, tm):
    n, d = x.shape
    groups, cg, _ = w_grp.shape
    halo_per_tile = tm // POOL_HALO
    return pl.pallas_call(
        functools.partial(_pool_mix_kernel, tiles_per_seq=seq // tm),
        grid=(n // tm,),
        in_specs=[
            pl.BlockSpec((tm, d), lambda i: (i, 0)),
            pl.BlockSpec((POOL_HALO, d), lambda i: (jnp.maximum(i * halo_per_tile - 1, 0), 0)),
            pl.BlockSpec((tm, d), lambda i: (i, 1)),
            pl.BlockSpec((tm, d), lambda i: (i, 0)),
            pl.BlockSpec((groups, cg, cg), lambda i: (0, 0, 0)),
            pl.BlockSpec((1, d), lambda i: (0, 0)),
            pl.BlockSpec((d, d), lambda i: (0, 0)),
            pl.BlockSpec((1, d), lambda i: (0, 0)),
            pl.BlockSpec((1, d), lambda i: (0, 0)),
        ],
        out_specs=pl.BlockSpec((tm, d), lambda i: (i, 0)),
        out_shape=jax.ShapeDtypeStruct((n, d), F32),
        scratch_shapes=[pltpu.VMEM((tm, d), BF16)],
        compiler_params=_params(("arbitrary",)),
        name="pool_mix",
    )(ug, ug, ug, x, w_grp, scale, w_out, g, b)


def _tile(n, preferred):
    t = preferred
    while n % t:
        t //= 2
    return t


def kernel(x, l0_ffn1_w_in, l0_ffn1_w_out, l0_ln1_g, l0_ln1_b, l0_sb_w_qkv, l0_sb_w_o, l0_ln2_g, l0_ln2_b, l0_ffn2_w_in, l0_ffn2_w_out, l0_ln3_g, l0_ln3_b, l1_ffn1_w_in, l1_ffn1_w_out, l1_ln1_g, l1_ln1_b, l1_pool_w_in, l1_pool_w_grp, l1_pool_scale, l1_pool_w_out, l1_ln2_g, l1_ln2_b, l1_ffn2_w_in, l1_ffn2_w_out, l1_ln3_g, l1_ln3_b):
    bsz, seq, d = x.shape
    n = bsz * seq
    bf = lambda w: w.astype(BF16)
    vec = lambda p: p.reshape(1, -1).astype(F32)

    tf = _tile(l0_ffn1_w_out.shape[0], 1024)
    ffn = functools.partial(_ffn_ln, tm=_tile(n, 512), tf=tf)
    h = x.reshape(n, d)

    h, w_in, w_out = ffn(h, _interleave_value_gate(l0_ffn1_w_in, tf), bf(l0_ffn1_w_out),
                         vec(l0_ln1_g), vec(l0_ln1_b), l0_ffn2_w_in, l0_ffn2_w_out)
    qkv = _proj(h, bf(l0_sb_w_qkv), BF16, tm=_tile(n, 1024), tn=_tile(l0_sb_w_qkv.shape[1], 1536))
    o = _sb_attention(qkv, bsz=bsz, seq=seq, q_blk=_tile(seq, 1024), t_blk=_tile(seq, 256), r_blk=128)
    h = _proj_res_ln(o, bf(l0_sb_w_o), h, vec(l0_ln2_g), vec(l0_ln2_b), tm=_tile(n, 512))
    h, w_in, w_out = ffn(h, w_in, w_out, vec(l0_ln3_g), vec(l0_ln3_b), l1_ffn1_w_in, l1_ffn1_w_out)

    h, w_in, w_out = ffn(h, w_in, w_out, vec(l1_ln1_g), vec(l1_ln1_b), l1_ffn2_w_in, l1_ffn2_w_out)
    ug = _proj(h, bf(l1_pool_w_in), F32, tm=_tile(n, 1024), tn=_tile(l1_pool_w_in.shape[1], 2048))
    h = _pool_mix(ug, h, bf(l1_pool_w_grp), vec(l1_pool_scale), bf(l1_pool_w_out), vec(l1_ln2_g), vec(l1_ln2_b),
                  seq=seq, tm=_tile(seq, 256))
    h, = ffn(h, w_in, w_out, vec(l1_ln3_g), vec(l1_ln3_b))
    return h.reshape(bsz, seq, d)
```

```python
import functools
import math

import jax
import jax.numpy as jnp
from jax import lax
from jax.experimental import pallas as pl
from jax.experimental.pallas import tpu as pltpu

F32 = jnp.float32
BF16 = jnp.bfloat16

DEPTH = 2
DEEPNORM_ALPHA = (2 * DEPTH) ** 0.25
FFN_RESIDUAL = 0.5
LN_EPS = 1e-5
SB_HEADS = 8
SB_HEADS_PER_STEP = 2
POOL_WINDOWS = (2, 4, 8, 16)
POOL_HALO = max(POOL_WINDOWS)

LOG2E = math.log2(math.e)
F32_EXP_ZERO_BELOW = -104.0
CUTOFF_MARGIN = 1.0
NO_KEYS = 1e30
NORM_SLACK = 1.01

VMEM_LIMIT = 56 * 1024 * 1024


def _params(semantics):
    return pltpu.CompilerParams(dimension_semantics=semantics, vmem_limit_bytes=VMEM_LIMIT)


def _layer_norm(y, g, b, eps=LN_EPS):
    mu = jnp.mean(y, axis=-1, keepdims=True)
    d = y - mu
    var = jnp.mean(d * d, axis=-1, keepdims=True)
    return d * lax.rsqrt(var + eps) * g + b


def _silu(a):
    return a * jax.nn.sigmoid(a)


def _dot(a, b):
    return jnp.dot(a, b, preferred_element_type=F32)


def _bits(x):
    return lax.bitcast_convert_type(x, jnp.uint32)


def _split_bf16(x):
    hi = lax.bitcast_convert_type(_bits(x) & jnp.uint32(0xFFFF0000), F32)
    return hi.astype(BF16), (x - hi).astype(BF16)


def _neg_abs(x):
    return lax.bitcast_convert_type(_bits(x) | jnp.uint32(0x80000000), F32)


def _cast_value_gate_interleaved(src, dst, tf):
    d_ff = src.shape[1] // 2
    for c in range(d_ff // tf):
        dst[:, 2 * c * tf:(2 * c + 1) * tf] = src[:, c * tf:(c + 1) * tf].astype(BF16)
        dst[:, (2 * c + 1) * tf:(2 * c + 2) * tf] = src[:, d_ff + c * tf:d_ff + (c + 1) * tf].astype(BF16)


def _interleave_value_gate(w_in, tf, *, row_block=256):
    r, cols = w_in.shape
    blk = _tile(r, row_block)
    spec = pl.BlockSpec((blk, cols), lambda i: (i, 0))
    return pl.pallas_call(
        functools.partial(_cast_value_gate_interleaved, tf=tf),
        grid=(r // blk,),
        in_specs=[spec],
        out_specs=spec,
        out_shape=jax.ShapeDtypeStruct(w_in.shape, BF16),
        compiler_params=_params(("arbitrary",)),
        name="cast_w_in",
    )(w_in)


def _ffn_ln_kernel(*refs, n_tiles, cast_interleave, tf):
    n_cast = len(cast_interleave)
    x_ref, w_ref, wo_ref, g_ref, b_ref = refs[:5]
    cast_src = refs[5:5 + n_cast]
    o_ref = refs[5 + n_cast]
    cast_dst = refs[6 + n_cast:6 + 2 * n_cast]
    xb_ref, acc_ref = refs[6 + 2 * n_cast:]
    i = pl.program_id(0)
    j = pl.program_id(1)
    cur = i % 2
    rows = x_ref.shape[0] // pl.num_programs(1)

    def normalise_slice_of_previous_tile():
        a = acc_ref[1 - cur, pl.ds(pl.multiple_of(j * rows, rows), rows), :]
        o_ref[...] = _layer_norm(a, g_ref[...], b_ref[...], eps=LN_EPS / FFN_RESIDUAL ** 2)

    def ride_along_cast(src, dst, interleave):
        if interleave:
            _cast_value_gate_interleaved(src, dst, tf)
        else:
            dst[...] = src[...].astype(BF16)

    @pl.when(i < n_tiles)
    def _():
        @pl.when(j == 0)
        def _():
            x = x_ref[...]
            xb_ref[...] = x.astype(BF16)
            acc_ref[cur] = (DEEPNORM_ALPHA / FFN_RESIDUAL) * x

        @pl.when(jnp.logical_and(i == 0, j == 0))
        def _():
            acc_ref[1] = jnp.zeros(acc_ref.shape[1:], F32)

        xb = xb_ref[...]
        a = _dot(xb, w_ref[:, :tf])
        normalise_slice_of_previous_tile()
        gate = _dot(xb, w_ref[:, tf:])
        h = (_silu(a) * gate).astype(BF16)
        acc_ref[cur] += _dot(h, wo_ref[...])
        for src, dst, interleave in zip(cast_src, cast_dst, cast_interleave):
            ride_along_cast(src, dst, interleave)

    @pl.when(i == n_tiles)
    def _():
        normalise_slice_of_previous_tile()


BF16_SUBLANES = 16
CAST_STEPS_PER_BLOCK = 2


def _ffn_ln(x, w_in, w_out, g, b, next_w_in=None, next_w_out=None, *, tm, tf):
    n, d = x.shape
    d_ff = w_out.shape[0]
    nj = d_ff // tf
    n_tiles = n // tm
    steps = n_tiles * nj
    assert tm % nj == 0 and (tm // nj) % 8 == 0
    cast = () if next_w_in is None else (next_w_in, next_w_out)
    cast_interleave = () if next_w_in is None else (True, False)

    def cast_spec(c):
        r, cols = c.shape
        blk = -(-r * CAST_STEPS_PER_BLOCK // (steps * BF16_SUBLANES)) * BF16_SUBLANES
        assert r % blk == 0
        nb = r // blk
        return pl.BlockSpec((blk, cols), lambda i, j: (jnp.minimum((i * nj + j) * nb // steps, nb - 1), 0))

    cast_specs = [cast_spec(c) for c in cast]
    outs = pl.pallas_call(
        functools.partial(_ffn_ln_kernel, n_tiles=n_tiles, cast_interleave=cast_interleave, tf=tf),
        grid=(n_tiles + 1, nj),
        in_specs=[
            pl.BlockSpec((tm, d), lambda i, j: (jnp.minimum(i, n_tiles - 1), 0)),
            pl.BlockSpec((d, 2 * tf), lambda i, j: (0, j)),
            pl.BlockSpec((tf, d), lambda i, j: (j, 0)),
            pl.BlockSpec((1, d), lambda i, j: (0, 0)),
            pl.BlockSpec((1, d), lambda i, j: (0, 0)),
        ] + cast_specs,
        out_specs=[pl.BlockSpec((tm // nj, d), lambda i, j: (jnp.maximum((i - 1) * nj + j, 0), 0))] + cast_specs,
        out_shape=[jax.ShapeDtypeStruct((n, d), F32)] + [jax.ShapeDtypeStruct(c.shape, BF16) for c in cast],
        scratch_shapes=[pltpu.VMEM((tm, d), BF16), pltpu.VMEM((2, tm, d), F32)],
        compiler_params=_params(("arbitrary", "arbitrary")),
        name="ffn_ln",
    )(x, w_in, w_out, g, b, *cast)
    return tuple(outs)


def _proj_kernel(x_ref, w_ref, o_ref, xb_ref):
    @pl.when(pl.program_id(1) == 0)
    def _():
        xb_ref[...] = x_ref[...].astype(BF16)

    o_ref[...] = _dot(xb_ref[...], w_ref[...]).astype(o_ref.dtype)


def _proj(x, w, out_dtype, *, tm, tn):
    n, d = x.shape
    n_out = w.shape[1]
    return pl.pallas_call(
        _proj_kernel,
        grid=(n // tm, n_out // tn),
        in_specs=[
            pl.BlockSpec((tm, d), lambda i, j: (i, 0)),
            pl.BlockSpec((d, tn), lambda i, j: (0, j)),
        ],
        out_specs=pl.BlockSpec((tm, tn), lambda i, j: (i, j)),
        out_shape=jax.ShapeDtypeStruct((n, n_out), out_dtype),
        scratch_shapes=[pltpu.VMEM((tm, d), BF16)],
        compiler_params=_params(("arbitrary", "arbitrary")),
        name="proj",
    )(x, w)


def _sb_attn_kernel(q_ref, k_ref, v_ref, o_ref, kmax2_ref, *, t_blk, r_blk, dh, scale):
    qi = pl.program_id(2)
    n_kblk = k_ref.shape[0] // t_blk
    n_sub = q_ref.shape[0] // t_blk
    heads = range(q_ref.shape[1] // dh)
    chunks = range(t_blk // r_blk)
    jobs = [(s, hh, ci) for s in range(n_sub) for hh in heads for ci in chunks]
    scale2 = scale * LOG2E
    cutoff2 = (F32_EXP_ZERO_BELOW - CUTOFF_MARGIN) * LOG2E

    def lanes(hh):
        return slice(hh * dh, (hh + 1) * dh)

    def rows(s, ci):
        r0 = s * t_blk + ci * r_blk
        return slice(r0, r0 + r_blk)

    def key_tile(s, p):
        return qi * n_sub + s - p

    def key_start(s, p):
        return pl.multiple_of(jnp.maximum(key_tile(s, p), 0) * t_blk, t_blk)

    def out_of_keys(s, p):
        return jnp.where(key_tile(s, p) < 0, jnp.float32(NO_KEYS), jnp.float32(0.0))

    @pl.when(qi == 0)
    def _():
        ones = jnp.ones((dh, dh), BF16)

        def body(i, ms):
            k_rows = pl.ds(pl.multiple_of(i * t_blk, t_blk), t_blk)
            ks = [k_ref[k_rows, lanes(hh)] for hh in heads]
            return tuple(jnp.maximum(m, _dot(k * k, ones)) for m, k in zip(ms, ks))

        ms = lax.fori_loop(0, n_kblk, body, tuple(jnp.zeros((t_blk, dh), F32) for _ in heads),
                           unroll=math.gcd(n_kblk, 8))
        for hh in heads:
            kmax2_ref[hh] = jnp.max(ms[hh]) * NORM_SLACK

    def iota2(shape, axis):
        return lax.broadcasted_iota(jnp.int32, shape, axis)

    def suffix_ones(kw):
        return (iota2((kw, kw), 0) >= iota2((kw, kw), 1)).astype(BF16)

    ones_full = suffix_ones(t_blk)

    def z_bound2(s, hh, ci):
        qf = q_ref[rows(s, ci), lanes(hh)].astype(F32)
        return jnp.sqrt(jnp.sum(qf * qf, axis=-1, keepdims=True) * kmax2_ref[hh]) * scale2

    def key_width(ci, diagonal):
        return min(t_blk, -(-((ci + 1) * r_blk) // dh) * dh) if diagonal else t_blk

    def phase_scores(p, diagonal):
        ys = []
        for s, hh, ci in jobs:
            q = q_ref[rows(s, ci), lanes(hh)]
            k = k_ref[pl.ds(key_start(s, p), key_width(ci, diagonal)), lanes(hh)]
            ys.append(lax.dot_general(q, k, (((1,), (1,)), ((), ())), preferred_element_type=F32) * scale2)
        return ys

    def phase_tails(ys, diagonal):
        pasts, tails = [], []
        for (s, hh, ci), y in zip(jobs, ys):
            kw = key_width(ci, diagonal)
            sp = jnp.maximum(y, 0.0) + jnp.log2(1.0 + jnp.exp2(_neg_abs(y)))
            if diagonal:
                pasts.append(iota2((r_blk, kw), 1) < iota2((r_blk, kw), 0) + ci * r_blk)
                sp = jnp.where(pasts[-1], sp, 0.0)
            hi, lo = _split_bf16(sp)
            ones = ones_full if kw == t_blk else suffix_ones(kw)
            tails.append(_dot(hi, ones) + _dot(lo, ones))
        return tails, pasts

    def phase_output(p, ys, tails, pasts, cs, accs, diagonal):
        new_cs, new_accs = [], []
        for n, (s, hh, ci) in enumerate(jobs):
            c = cs[n] if diagonal else cs[n] + out_of_keys(s, p)
            w = jnp.exp2(jnp.minimum(ys[n] - tails[n] - c, 0.0))
            if diagonal:
                w = jnp.where(pasts[n], w, 0.0)
            v = v_ref[pl.ds(key_start(s, p), key_width(ci, diagonal)), lanes(hh)]
            new_accs.append(accs[n] + _dot(w.astype(BF16), v))
            new_cs.append(c + tails[n][:, 0:1])
        return tuple(new_cs), tuple(new_accs)

    def live(cs):
        worst = functools.reduce(jnp.maximum, [z_bound2(*job) - cs[n] for n, job in enumerate(jobs)])
        return jnp.max(worst) >= cutoff2

    y_d = phase_scores(0, True)
    y_p = phase_scores(1, False)
    tails_d, pasts_d = phase_tails(y_d, True)
    tails_p, _ = phase_tails(y_p, False)
    cs = tuple(jnp.zeros((r_blk, 1), F32) for _ in jobs)
    accs = tuple(jnp.zeros((r_blk, dh), F32) for _ in jobs)
    cs, accs = phase_output(0, y_d, tails_d, pasts_d, cs, accs, True)
    cs, accs = phase_output(1, y_p, tails_p, None, cs, accs, False)

    def cond(carry):
        p, go, _, _ = carry
        return jnp.logical_and(key_tile(n_sub - 1, p) >= 0, go)

    def body(carry):
        p, _, cs, accs = carry
        ys = phase_scores(p, False)
        tails, _ = phase_tails(ys, False)
        cs, accs = phase_output(p, ys, tails, None, cs, accs, False)
        return p + 1, live(cs), cs, accs

    _, _, _, accs = lax.while_loop(cond, body, (jnp.int32(2), live(cs), cs, accs))
    for n, (s, hh, ci) in enumerate(jobs):
        o_ref[rows(s, ci), lanes(hh)] = accs[n].astype(o_ref.dtype)


def _sb_attention(qkv, *, bsz, seq, q_blk, t_blk, r_blk):
    n, three_inner = qkv.shape
    inner = three_inner // 3
    dh = inner // SB_HEADS
    hp = SB_HEADS_PER_STEP
    groups = SB_HEADS // hp
    nq = seq // q_blk
    return pl.pallas_call(
        functools.partial(_sb_attn_kernel, t_blk=t_blk, r_blk=r_blk, dh=dh, scale=dh ** -0.5),
        grid=(bsz, groups, nq),
        in_specs=[
            pl.BlockSpec((q_blk, hp * dh), lambda b, h, i: (b * nq + i, h)),
            pl.BlockSpec((seq, hp * dh), lambda b, h, i: (b, groups + h)),
            pl.BlockSpec((seq, hp * dh), lambda b, h, i: (b, 2 * groups + h)),
        ],
        out_specs=pl.BlockSpec((q_blk, hp * dh), lambda b, h, i: (b * nq + i, h)),
        out_shape=jax.ShapeDtypeStruct((n, inner), BF16),
        scratch_shapes=[pltpu.SMEM((hp,), F32)],
        compiler_params=_params(("arbitrary", "arbitrary", "arbitrary")),
        name="sb_attn",
    )(qkv, qkv, qkv)


def _proj_res_ln_kernel(o_ref, w_ref, x_ref, g_ref, b_ref, out_ref):
    y = DEEPNORM_ALPHA * x_ref[...] + _dot(o_ref[...], w_ref[...])
    out_ref[...] = _layer_norm(y, g_ref[...], b_ref[...])


def _proj_res_ln(o, w, x, g, b, *, tm):
    n, d = x.shape
    inner = o.shape[1]
    return pl.pallas_call(
        _proj_res_ln_kernel,
        grid=(n // tm,),
        in_specs=[
            pl.BlockSpec((tm, inner), lambda i: (i, 0)),
            pl.BlockSpec((inner, d), lambda i: (0, 0)),
            pl.BlockSpec((tm, d), lambda i: (i, 0)),
            pl.BlockSpec((1, d), lambda i: (0, 0)),
            pl.BlockSpec((1, d), lambda i: (0, 0)),
        ],
        out_specs=pl.BlockSpec((tm, d), lambda i: (i, 0)),
        out_shape=jax.ShapeDtypeStruct((n, d), F32),
        compiler_params=_params(("arbitrary",)),
        name="proj_res_ln",
    )(o, w, x, g, b)


def _pool_mix_kernel(u_ref, halo_ref, gate_ref, x_ref, wgrp_ref, scale_ref, wout_ref, g_ref, b_ref,
                     out_ref, t_ref, *, tiles_per_seq):
    tm, d = x_ref.shape
    cg = d // len(POOL_WINDOWS)
    tile_in_seq = pl.program_id(0) % tiles_per_seq

    pos = tile_in_seq * tm + lax.broadcasted_iota(jnp.int32, (tm, 1), 0)
    halo_in_seq = tile_in_seq * tm - POOL_HALO + lax.broadcasted_iota(jnp.int32, (POOL_HALO, 1), 0) >= 0

    for gi, window in enumerate(POOL_WINDOWS):
        cols = slice(gi * cg, (gi + 1) * cg)
        u = u_ref[:, cols]
        halo = jnp.where(halo_in_seq, halo_ref[:, cols], 0.0)
        s = jnp.concatenate([halo, u], axis=0)
        span = 1
        while span < window:
            s = s + pltpu.roll(s, span, axis=0)
            span *= 2
        count = jnp.minimum(pos + 1, window).astype(F32)
        pooled = s[POOL_HALO:] / count - u
        mixed = _dot(pooled.astype(BF16), wgrp_ref[gi])
        t_ref[:, cols] = (scale_ref[:, cols] * mixed * _silu(gate_ref[:, cols])).astype(BF16)

    y = DEEPNORM_ALPHA * x_ref[...] + _dot(t_ref[...], wout_ref[...])
    out_ref[...] = _layer_norm(y, g_ref[...], b_ref[...])


def _pool_mix(ug, x, w_grp, scale, w_out, g, b, *, seq, tm):
    n, d = x.shape
    groups, cg, _ = w_grp.shape
    halo_per_tile = tm // POOL_HALO
    return pl.pallas_call(
        functools.partial(_pool_mix_kernel, tiles_per_seq=seq // tm),
        grid=(n // tm,),
        in_specs=[
            pl.BlockSpec((tm, d), lambda i: (i, 0)),
            pl.BlockSpec((POOL_HALO, d), lambda i: (jnp.maximum(i * halo_per_tile - 1, 0), 0)),
            pl.BlockSpec((tm, d), lambda i: (i, 1)),
            pl.BlockSpec((tm, d), lambda i: (i, 0)),
            pl.BlockSpec((groups, cg, cg), lambda i: (0, 0, 0)),
            pl.BlockSpec((1, d), lambda i: (0, 0)),
            pl.BlockSpec((d, d), lambda i: (0, 0)),
            pl.BlockSpec((1, d), lambda i: (0, 0)),
            pl.BlockSpec((1, d), lambda i: (0, 0)),
        ],
        out_specs=pl.BlockSpec((tm, d), lambda i: (i, 0)),
        out_shape=jax.ShapeDtypeStruct((n, d), F32),
        scratch_shapes=[pltpu.VMEM((tm, d), BF16)],
        compiler_params=_params(("arbitrary",)),
        name="pool_mix",
    )(ug, ug, ug, x, w_grp, scale, w_out, g, b)


def _tile(n, preferred):
    t = preferred
    while n % t:
        t //= 2
    return t


def kernel(x, l0_ffn1_w_in, l0_ffn1_w_out, l0_ln1_g, l0_ln1_b, l0_sb_w_qkv, l0_sb_w_o, l0_ln2_g, l0_ln2_b, l0_ffn2_w_in, l0_ffn2_w_out, l0_ln3_g, l0_ln3_b, l1_ffn1_w_in, l1_ffn1_w_out, l1_ln1_g, l1_ln1_b, l1_pool_w_in, l1_pool_w_grp, l1_pool_scale, l1_pool_w_out, l1_ln2_g, l1_ln2_b, l1_ffn2_w_in, l1_ffn2_w_out, l1_ln3_g, l1_ln3_b):
    bsz, seq, d = x.shape
    n = bsz * seq
    bf = lambda w: w.astype(BF16)
    vec = lambda p: p.reshape(1, -1).astype(F32)

    tf = _tile(l0_ffn1_w_out.shape[0], 1024)
    ffn = functools.partial(_ffn_ln, tm=_tile(n, 512), tf=tf)
    h = x.reshape(n, d)

    h, w_in, w_out = ffn(h, _interleave_value_gate(l0_ffn1_w_in, tf), bf(l0_ffn1_w_out),
                         vec(l0_ln1_g), vec(l0_ln1_b), l0_ffn2_w_in, l0_ffn2_w_out)
    qkv = _proj(h, bf(l0_sb_w_qkv), BF16, tm=_tile(n, 1024), tn=_tile(l0_sb_w_qkv.shape[1], 1536))
    o = _sb_attention(qkv, bsz=bsz, seq=seq, q_blk=_tile(seq, 1024), t_blk=_tile(seq, 256), r_blk=128)
    h = _proj_res_ln(o, bf(l0_sb_w_o), h, vec(l0_ln2_g), vec(l0_ln2_b), tm=_tile(n, 1024))
    h, w_in, w_out = ffn(h, w_in, w_out, vec(l0_ln3_g), vec(l0_ln3_b), l1_ffn1_w_in, l1_ffn1_w_out)

    h, w_in, w_out = ffn(h, w_in, w_out, vec(l1_ln1_g), vec(l1_ln1_b), l1_ffn2_w_in, l1_ffn2_w_out)
    ug = _proj(h, bf(l1_pool_w_in), F32, tm=_tile(n, 1024), tn=_tile(l1_pool_w_in.shape[1], 2048))
    h = _pool_mix(ug, h, bf(l1_pool_w_grp), vec(l1_pool_scale), bf(l1_pool_w_out), vec(l1_ln2_g), vec(l1_ln2_b),
                  seq=seq, tm=_tile(seq, 256))
    h, = ffn(h, w_in, w_out, vec(l1_ln3_g), vec(l1_ln3_b))
    return h.reshape(bsz, seq, d)
```
